```python
import math
import jax, jax.numpy as jnp
from jax import lax
import numpy as np


D_MODEL = 1024
BATCH = 2
SEQ = 16384
DEPTH = 2

N_MIXERS = 2
N_GLA_LAYERS = (DEPTH + 1) // 2
N_DIFF_LAYERS = DEPTH // 2
RMS_EPS = 1e-6

GLA_HEADS = 4
GLA_DK = D_MODEL // 2 // GLA_HEADS
GLA_DV = D_MODEL // GLA_HEADS
GLA_GATE_RANK = 16
GLA_TAU = 16.0
GLA_CHUNK = 64
GLA_QK = GLA_HEADS * GLA_DK
GLA_V = GLA_HEADS * GLA_DV
GLA_IN = 2 * GLA_QK + 2 * GLA_V + GLA_GATE_RANK

DIFF_HEADS = D_MODEL // 128
DIFF_DH = 64
DIFF_DV = 2 * DIFF_DH
DIFF_QK = 2 * DIFF_HEADS * DIFF_DH
DIFF_VW = DIFF_HEADS * DIFF_DV
DIFF_IN = 2 * DIFF_QK + DIFF_VW
ROPE_DIMS = DIFF_DH // 4
ROPE_THETA = 500000.0
Q_BLOCK = 128

PEER_HEADS = 8
PEER_NKEYS = 128
PEER_N_EXPERTS = PEER_NKEYS * PEER_NKEYS
PEER_DKEY = 128
PEER_TOPK = 16
PEER_TOKEN_BLOCK = 128

kernel_name = "hybrid_gla_diffattn_peer"


def rmsnorm(x, g):
    xf = x.astype(jnp.float32)
    y = xf * lax.rsqrt(jnp.mean(xf * xf, axis=-1, keepdims=True) + RMS_EPS)
    return (y * g.astype(jnp.float32)).astype(x.dtype)


def gla_mixer(h, w_in, w_gate_up, b_gate, g_norm, w_out):
    B, S, _ = h.shape
    nC = S // GLA_CHUNK
    proj = h @ w_in
    q, k, v, r, glr = jnp.split(
        proj, [GLA_QK, 2 * GLA_QK, 2 * GLA_QK + GLA_V, 2 * GLA_QK + 2 * GLA_V], axis=-1)
    glog = jax.nn.log_sigmoid((glr @ w_gate_up + b_gate).astype(jnp.float32)) / GLA_TAU

    def heads(t, d):
        return t.reshape(B, nC, GLA_CHUNK, GLA_HEADS, d).transpose(0, 3, 1, 2, 4)

    q = heads(q, GLA_DK) * (GLA_DK ** -0.5)
    k = heads(k, GLA_DK)
    v = heads(v, GLA_DV)
    cum = jnp.cumsum(heads(glog, GLA_DK), axis=3)
    last = cum[:, :, :, -1:, :]
    mid = cum[:, :, :, GLA_CHUNK // 2 - 1:GLA_CHUNK // 2, :]

    a = jnp.einsum('bhncd,bhnjd->bhncj', q * jnp.exp(cum - mid), k * jnp.exp(mid - cum))
    causal = jnp.tril(jnp.ones((GLA_CHUNK, GLA_CHUNK), dtype=bool))
    a = jnp.where(causal, a, 0.0)
    o_intra = jnp.einsum('bhncj,bhnjv->bhncv', a, v)

    kv = jnp.einsum('bhncd,bhncv->nbhdv', k * jnp.exp(last - cum), v)
    chunk_decay = jnp.exp(last[:, :, :, 0, :]).transpose(2, 0, 1, 3)

    def step(state, inp):
        dec, kv_c = inp
        return dec[..., None] * state + kv_c, state

    init = jnp.zeros((B, GLA_HEADS, GLA_DK, GLA_DV), kv.dtype)
    _, states = lax.scan(step, init, (chunk_decay, kv))
    o_inter = jnp.einsum('bhncd,nbhdv->bhncv', q * jnp.exp(cum), states)

    o = (o_intra + o_inter).transpose(0, 2, 3, 1, 4).reshape(B, S, GLA_HEADS, GLA_DV)
    o = rmsnorm(o, g_norm).reshape(B, S, GLA_V) * jax.nn.silu(r.astype(jnp.float32))
    return o.astype(h.dtype) @ w_out


def rope_partial(t, cos, sin):
    half = ROPE_DIMS // 2
    x1, x2, rest = t[..., :half], t[..., half:ROPE_DIMS], t[..., ROPE_DIMS:]
    return jnp.concatenate([x1 * cos - x2 * sin, x2 * cos + x1 * sin, rest], axis=-1).astype(t.dtype)


def diff_mixer(h, positions, w_in, lq1, lk1, lq2, lk2, g_norm, w_out, lambda_init):
    B, S, _ = h.shape
    nQ = S // Q_BLOCK
    proj = h @ w_in
    q, k, v = jnp.split(proj, [DIFF_QK, 2 * DIFF_QK], axis=-1)
    q = q.reshape(B, S, 2 * DIFF_HEADS, DIFF_DH)
    k = k.reshape(B, S, 2 * DIFF_HEADS, DIFF_DH)
    v = v.reshape(B, S, DIFF_HEADS, DIFF_DV)

    inv_freq = ROPE_THETA ** (-jnp.arange(0, ROPE_DIMS, 2, dtype=jnp.float32) / ROPE_DIMS)
    ang = positions.astype(jnp.float32)[..., None] * inv_freq
    cos, sin = jnp.cos(ang)[:, :, None, :], jnp.sin(ang)[:, :, None, :]
    q = rope_partial(q, cos, sin) * (DIFF_DH ** -0.5)
    k = rope_partial(k, cos, sin)
    q = q.reshape(B, S, DIFF_HEADS, 2, DIFF_DH)
    k = k.reshape(B, S, DIFF_HEADS, 2, DIFF_DH)

    lam = (jnp.exp(jnp.sum(lq1.astype(jnp.float32) * lk1.astype(jnp.float32)))
           - jnp.exp(jnp.sum(lq2.astype(jnp.float32) * lk2.astype(jnp.float32)))
           + lambda_init)

    qb = q.reshape(B, nQ, Q_BLOCK, DIFF_HEADS, 2, DIFF_DH).transpose(1, 0, 2, 3, 4, 5)
    key_idx = jnp.arange(S)

    def block(args):
        q_blk, i = args
        s = jnp.einsum('bqhmd,bkhmd->bhmqk', q_blk, k).astype(jnp.float32)
        q_idx = i * Q_BLOCK + jnp.arange(Q_BLOCK)
        mask = key_idx[None, :] <= q_idx[:, None]
        p = jax.nn.softmax(jnp.where(mask, s, -jnp.inf), axis=-1)
        attn = p[:, :, 0] - lam * p[:, :, 1]
        return jnp.einsum('bhqk,bkhv->bqhv', attn.astype(v.dtype), v)

    o = lax.map(block, (qb, jnp.arange(nQ)))
    o = o.transpose(1, 0, 2, 3, 4).reshape(B, S, DIFF_HEADS, DIFF_DV)
    o = rmsnorm(o, g_norm) * (1.0 - lambda_init)
    return o.reshape(B, S, DIFF_VW) @ w_out


def peer(h, w_query, sub_keys, expert_u, expert_v):
    B, S, D = h.shape
    tokens = h.reshape(-1, PEER_TOKEN_BLOCK, D)

    def block(x_blk):
        T = x_blk.shape[0]
        qry = (x_blk @ w_query).reshape(T, PEER_HEADS, 2, PEER_DKEY)
        scores = jnp.einsum('thpd,hpnd->thpn', qry, sub_keys).astype(jnp.float32)
        top_s, top_i = lax.top_k(scores, PEER_TOPK)
        cand_s = top_s[:, :, 0, :, None] + top_s[:, :, 1, None, :]
        cand_i = top_i[:, :, 0, :, None] * PEER_NKEYS + top_i[:, :, 1, None, :]
        cand_s = cand_s.reshape(T, PEER_HEADS, PEER_TOPK * PEER_TOPK)
        cand_i = cand_i.reshape(T, PEER_HEADS, PEER_TOPK * PEER_TOPK)
        best_s, best_pos = lax.top_k(cand_s, PEER_TOPK)
        idx = jnp.take_along_axis(cand_i, best_pos, axis=-1)
        gate = jax.nn.softmax(best_s, axis=-1)
        u = expert_u[idx]
        act = jax.nn.gelu(jnp.einsum('thkd,td->thk', u, x_blk), approximate=False)
        vsel = expert_v[idx]
        return jnp.einsum('thk,thkd->td', (gate * act).astype(vsel.dtype), vsel)

    return lax.map(block, tokens).reshape(B, S, D)


def setup_inputs(seed: int = 0) -> dict:
    key = jax.random.key(seed)
    ks = jax.random.split(key, 24)
    f32 = jnp.float32
    D = D_MODEL

    def nrm(k, shape, scale):
        return jax.random.normal(k, shape, f32) * scale

    x = jax.random.normal(ks[0], (BATCH, SEQ, D), f32)
    positions = jnp.broadcast_to(jnp.arange(SEQ, dtype=jnp.int32)[None, :], (BATCH, SEQ))
    return {
        "x": x,
        "positions": positions,
        "norm_mix": 1.0 + nrm(ks[1], (DEPTH, D), 0.02),
        "norm_ffn": 1.0 + nrm(ks[2], (DEPTH, D), 0.02),
        "norm_final": 1.0 + nrm(ks[3], (D,), 0.02),
        "gla_w_in": nrm(ks[4], (N_GLA_LAYERS, D, GLA_IN), D ** -0.5),
        "gla_w_gate_up": nrm(ks[5], (N_GLA_LAYERS, GLA_GATE_RANK, GLA_QK), GLA_GATE_RANK ** -0.5),
        "gla_b_gate": nrm(ks[6], (N_GLA_LAYERS, GLA_QK), 0.1),
        "gla_norm": 1.0 + nrm(ks[7], (N_GLA_LAYERS, GLA_HEADS, GLA_DV), 0.02),
        "gla_w_out": nrm(ks[8], (N_GLA_LAYERS, GLA_V, D), GLA_V ** -0.5),
        "diff_w_in": nrm(ks[9], (N_DIFF_LAYERS, D, DIFF_IN), D ** -0.5),
        "diff_lambda_q1": nrm(ks[10], (N_DIFF_LAYERS, DIFF_DH), 0.1),
        "diff_lambda_k1": nrm(ks[11], (N_DIFF_LAYERS, DIFF_DH), 0.1),
        "diff_lambda_q2": nrm(ks[12], (N_DIFF_LAYERS, DIFF_DH), 0.1),
        "diff_lambda_k2": nrm(ks[13], (N_DIFF_LAYERS, DIFF_DH), 0.1),
        "diff_norm": 1.0 + nrm(ks[14], (N_DIFF_LAYERS, DIFF_HEADS, DIFF_DV), 0.02),
        "diff_w_out": nrm(ks[15], (N_DIFF_LAYERS, DIFF_VW, D), DIFF_VW ** -0.5),
        "peer_w_query": nrm(ks[16], (DEPTH, D, PEER_HEADS * 2 * PEER_DKEY), D ** -0.5),
        "peer_sub_keys": nrm(ks[17], (DEPTH, PEER_HEADS, 2, PEER_NKEYS, PEER_DKEY), PEER_DKEY ** -0.5),
        "peer_u": nrm(ks[18], (DEPTH, PEER_N_EXPERTS, D), D ** -0.5),
        "peer_v": nrm(ks[19], (DEPTH, PEER_N_EXPERTS, D), PEER_HEADS ** -0.5),
    }


def reference(x, positions, norm_mix, norm_ffn, norm_final,
              gla_w_in, gla_w_gate_up, gla_b_gate, gla_norm, gla_w_out,
              diff_w_in, diff_lambda_q1, diff_lambda_k1, diff_lambda_q2, diff_lambda_k2,
              diff_norm, diff_w_out,
              peer_w_query, peer_sub_keys, peer_u, peer_v):
    h = x
    for i in range(DEPTH):
        hn = rmsnorm(h, norm_mix[i])
        j = i // N_MIXERS
        if i % N_MIXERS == 0:
            mix = gla_mixer(hn, gla_w_in[j], gla_w_gate_up[j], gla_b_gate[j],
                            gla_norm[j], gla_w_out[j])
        else:
            lambda_init = 0.8 - 0.6 * math.exp(-0.3 * i)
            mix = diff_mixer(hn, positions, diff_w_in[j], diff_lambda_q1[j], diff_lambda_k1[j],
                             diff_lambda_q2[j], diff_lambda_k2[j], diff_norm[j], diff_w_out[j],
                             lambda_init)
        h = h + mix.astype(h.dtype)
        ff = peer(rmsnorm(h, norm_ffn[i]), peer_w_query[i], peer_sub_keys[i], peer_u[i], peer_v[i])
        h = h + ff.astype(h.dtype)
    return rmsnorm(h, norm_final)
```

```python
import functools
import math

import jax
import jax.numpy as jnp
from jax import lax
from jax.experimental import pallas as pl
from jax.experimental.pallas import tpu as pltpu

D_MODEL = 1024
RMS_EPS = 1e-6

GLA_HEADS = 4
GLA_DK = 128
GLA_DV = 256
GLA_RANK = 16
GLA_TAU = 16.0
GLA_CHUNK = 64
GLA_QK = GLA_HEADS * GLA_DK
GLA_V = GLA_HEADS * GLA_DV
GLA_IN_PAD = 2 * GLA_QK + 2 * GLA_V + 128

DIFF_HEADS = 8
DIFF_DH = 64
DIFF_DV = 128
ROPE_DIMS = 16
ROPE_THETA = 500000.0

PEER_HEADS = 8
PEER_NKEYS = 128
PEER_TOPK = 16

V7X_VMEM_LIMIT_BYTES = 48 * 1024 * 1024
BF16 = jnp.bfloat16
F32 = jnp.float32
NT_DIMS = (((1,), (1,)), ((), ()))
TN_DIMS = (((0,), (0,)), ((), ()))


def _params(*sem):
    return pltpu.CompilerParams(dimension_semantics=sem,
                                vmem_limit_bytes=V7X_VMEM_LIMIT_BYTES)


def _rms_matmul_kernel(x_ref, g_ref, w_ref, o_ref, *rest, emit_xn):
    if emit_xn:
        xn_out_ref, xn_ref = rest
    else:
        (xn_ref,) = rest

    @pl.when(pl.program_id(1) == 0)
    def _():
        x = x_ref[...]
        y = x * lax.rsqrt(jnp.mean(x * x, axis=-1, keepdims=True) + RMS_EPS)
        xn_ref[...] = (y * g_ref[...]).astype(BF16)

    if emit_xn:
        xn_out_ref[...] = xn_ref[...]
    o_ref[...] = jnp.dot(xn_ref[...], w_ref[...], preferred_element_type=F32)


def rms_matmul(x, g, w, *, tm, tn, emit_xn=False):
    n, d = x.shape
    m = w.shape[1]
    out_shape = [jax.ShapeDtypeStruct((n, m), F32)]
    out_specs = [pl.BlockSpec((tm, tn), lambda i, j: (i, j))]
    if emit_xn:
        out_shape.append(jax.ShapeDtypeStruct((n, d), BF16))
        out_specs.append(pl.BlockSpec((tm, d), lambda i, j: (i, 0)))
    res = pl.pallas_call(
        functools.partial(_rms_matmul_kernel, emit_xn=emit_xn),
        grid=(n // tm, m // tn),
        in_specs=[pl.BlockSpec((tm, d), lambda i, j: (i, 0)),
                  pl.BlockSpec((1, d), lambda i, j: (0, 0)),
                  pl.BlockSpec((d, tn), lambda i, j: (0, j))],
        out_specs=out_specs,
        out_shape=out_shape,
        scratch_shapes=[pltpu.VMEM((tm, d), BF16)],
        compiler_params=_params("parallel", "arbitrary"),
        name="rms_matmul_xn" if emit_xn else "rms_matmul",
    )(x, g.reshape(1, d), w)
    return res if emit_xn else res[0]


def _matmul_res_kernel(a_ref, w_ref, r_ref, o_ref):
    o_ref[...] = r_ref[...] + jnp.dot(a_ref[...], w_ref[...], preferred_element_type=F32)


def matmul_res(a, w, res, *, tm, tn):
    n, k = a.shape
    m = w.shape[1]
    return pl.pallas_call(
        _matmul_res_kernel,
        grid=(n // tm, m // tn),
        in_specs=[pl.BlockSpec((tm, k), lambda i, j: (i, 0)),
                  pl.BlockSpec((k, tn), lambda i, j: (0, j)),
                  pl.BlockSpec((tm, tn), lambda i, j: (i, j))],
        out_specs=pl.BlockSpec((tm, tn), lambda i, j: (i, j)),
        out_shape=jax.ShapeDtypeStruct((n, m), F32),
        compiler_params=_params("parallel", "arbitrary"),
        name="matmul_res",
    )(a, w, res)


GLA_STEP = 512


def _gla_kernel(q_ref, k_ref, v_ref, r_ref, glr_ref, wg_ref, bg_ref, gn_ref,
                o_ref, state_ref):
    @pl.when(pl.program_id(2) == 0)
    def _():
        state_ref[...] = jnp.zeros_like(state_ref)

    c = GLA_CHUNK
    wg = wg_ref[...].astype(BF16)
    bg = bg_ref[...]
    gn = gn_ref[0]
    row = lax.broadcasted_iota(jnp.int32, (c, c), 0)
    col = lax.broadcasted_iota(jnp.int32, (c, c), 1)
    causal = col <= row
    tril = causal.astype(BF16)

    def chunk(ci, carry):
        sl = pl.ds(pl.multiple_of(ci * c, c), c)
        q = q_ref[sl, :] * (GLA_DK ** -0.5)
        k = k_ref[sl, :]
        v = v_ref[sl, :].astype(BF16)
        z = jnp.dot(glr_ref[sl, :].astype(BF16), wg, preferred_element_type=F32) + bg
        glog = jax.nn.log_sigmoid(z) / GLA_TAU
        g_hi = glog.astype(BF16)
        g_lo = (glog - g_hi.astype(F32)).astype(BF16)
        cum = (jnp.dot(tril, g_hi, preferred_element_type=F32)
               + jnp.dot(tril, g_lo, preferred_element_type=F32))
        last = cum[c - 1:c, :]
        mid = cum[c // 2 - 1:c // 2, :]
        qe = (q * jnp.exp(cum - mid)).astype(BF16)
        ke = (k * jnp.exp(mid - cum)).astype(BF16)
        a = lax.dot_general(qe, ke, NT_DIMS, preferred_element_type=F32)
        a = jnp.where(causal, a, 0.0).astype(BF16)
        o = jnp.dot(a, v, preferred_element_type=F32)
        st = state_ref[...]
        qc = (q * jnp.exp(cum)).astype(BF16)
        o = o + lax.dot_general(qc, st.astype(BF16), NT_DIMS, preferred_element_type=F32)
        kd = (k * jnp.exp(last - cum)).astype(BF16)
        kv_t = lax.dot_general(v, kd, TN_DIMS, preferred_element_type=F32)
        state_ref[...] = st * jnp.exp(last) + kv_t
        y = o * lax.rsqrt(jnp.mean(o * o, axis=-1, keepdims=True) + RMS_EPS) * gn
        r = r_ref[sl, :]
        o_ref[sl, :] = (y * (r * jax.nn.sigmoid(r))).astype(o_ref.dtype)
        return carry

    lax.fori_loop(0, GLA_STEP // c, chunk, 0)


def gla_core(proj, w_gate_up_pad, b_gate, g_norm, *, batch, seq):
    n = proj.shape[0]
    steps = seq // GLA_STEP
    h_ = GLA_HEADS

    def rows(b, h, s):
        return b * steps + s

    return pl.pallas_call(
        _gla_kernel,
        grid=(batch, h_, steps),
        in_specs=[
            pl.BlockSpec((GLA_STEP, GLA_DK), lambda b, h, s: (rows(b, h, s), h)),
            pl.BlockSpec((GLA_STEP, GLA_DK), lambda b, h, s: (rows(b, h, s), h_ + h)),
            pl.BlockSpec((GLA_STEP, GLA_DV), lambda b, h, s: (rows(b, h, s), h_ + h)),
            pl.BlockSpec((GLA_STEP, GLA_DV), lambda b, h, s: (rows(b, h, s), 2 * h_ + h)),
            pl.BlockSpec((GLA_STEP, 128), lambda b, h, s: (rows(b, h, s), (2 * GLA_QK + 2 * GLA_V) // 128)),
            pl.BlockSpec((128, GLA_DK), lambda b, h, s: (0, h)),
            pl.BlockSpec((1, GLA_DK), lambda b, h, s: (0, h)),
            pl.BlockSpec((1, 1, GLA_DV), lambda b, h, s: (h, 0, 0)),
        ],
        out_specs=pl.BlockSpec((GLA_STEP, GLA_DV), lambda b, h, s: (rows(b, h, s), h)),
        out_shape=jax.ShapeDtypeStruct((n, GLA_V), BF16),
        scratch_shapes=[pltpu.VMEM((GLA_DV, GLA_DK), F32)],
        compiler_params=_params("parallel", "parallel", "arbitrary"),
        name="gla_core",
    )(proj, proj, proj, proj, proj, w_gate_up_pad, b_gate.reshape(1, GLA_QK),
      g_norm.reshape(GLA_HEADS, 1, GLA_DV))


def _rope_kernel(pos_ref, freq_ref, q_ref, k_ref, v_ref, qo_ref, ko_ref, vo_ref):
    ang = pos_ref[...].astype(F32) * freq_ref[0:1, :]
    cos = jnp.cos(ang)
    sin = jnp.sin(ang)
    sgn_lo = freq_ref[1:2, :]
    sgn_hi = freq_ref[2:3, :]
    half = ROPE_DIMS // 2

    def rot(t, scale):
        outs = []
        for h in range(DIFF_HEADS):
            x = t[:, h * 128:(h + 1) * 128]
            partner = (sgn_lo * pltpu.roll(x, 128 - half, 1)
                       + sgn_hi * pltpu.roll(x, half, 1))
            outs.append(((x * cos + partner * sin) * scale).astype(BF16))
        return jnp.concatenate(outs, axis=1)

    qo_ref[...] = rot(q_ref[...], DIFF_DH ** -0.5)
    ko_ref[...] = rot(k_ref[...], 1.0)
    vo_ref[...] = v_ref[...].astype(BF16)


def rope_prep(proj, positions, *, tm):
    n = proj.shape[0]
    w = DIFF_HEADS * 128
    lane = jnp.arange(128) % DIFF_DH
    inv_freq = ROPE_THETA ** (-jnp.arange(0, ROPE_DIMS, 2, dtype=F32) / ROPE_DIMS)
    half = ROPE_DIMS // 2
    freq = jnp.where(lane < ROPE_DIMS, inv_freq[lane % half], 0.0)
    sgn_lo = jnp.where(lane < half, -1.0, 0.0)
    sgn_hi = jnp.where((lane >= half) & (lane < ROPE_DIMS), 1.0, 0.0)
    table = jnp.zeros((8, 128), F32).at[0].set(freq).at[1].set(sgn_lo).at[2].set(sgn_hi)
    spec_in = lambda c: pl.BlockSpec((tm, w), lambda i: (i, c))
    out = jax.ShapeDtypeStruct((n, w), BF16)
    return pl.pallas_call(
        _rope_kernel,
        grid=(n // tm,),
        in_specs=[pl.BlockSpec((tm, 1), lambda i: (i, 0)),
                  pl.BlockSpec((8, 128), lambda i: (0, 0)),
                  spec_in(0), spec_in(1), spec_in(2)],
        out_specs=[pl.BlockSpec((tm, w), lambda i: (i, 0))] * 3,
        out_shape=[out, out, out],
        compiler_params=_params("parallel"),
        name="rope_prep",
    )(positions.reshape(n, 1), table, proj, proj, proj)


ATT_T = 512


def _flash_kernel(qi_ref, kj_ref, lam_ref, q_ref, k_ref, v_ref, gn_ref, o_ref,
                  m_ref, l_ref, acc_ref, *, lambda_init):
    step = pl.program_id(2)
    qi = qi_ref[step]
    kj = kj_ref[step]
    t = ATT_T

    @pl.when(kj == 0)
    def _():
        m_ref[...] = jnp.full_like(m_ref, -jnp.inf)
        l_ref[...] = jnp.zeros_like(l_ref)
        acc_ref[...] = jnp.zeros_like(acc_ref)

    q = q_ref[...]
    k = k_ref[...]
    v = v_ref[...]
    lane = lax.broadcasted_iota(jnp.int32, k.shape, 1)
    row = lax.broadcasted_iota(jnp.int32, (t, t), 0) + qi * t
    col = lax.broadcasted_iota(jnp.int32, (t, t), 1) + kj * t
    visible = col <= row
    for m in range(2):
        in_map = (lane < DIFF_DH) if m == 0 else (lane >= DIFF_DH)
        km = jnp.where(in_map, k, jnp.zeros_like(k))
        s = lax.dot_general(q, km, NT_DIMS, preferred_element_type=F32)
        s = jnp.where(visible, s, -jnp.inf)
        m_old = m_ref[m]
        m_new = jnp.maximum(m_old, jnp.max(s, axis=1, keepdims=True))
        p = jnp.exp(s - m_new)
        alpha = jnp.exp(m_old - m_new)
        l_ref[m] = alpha * l_ref[m] + jnp.sum(p, axis=1, keepdims=True)
        acc_ref[m] = alpha * acc_ref[m] + jnp.dot(p.astype(BF16), v, preferred_element_type=F32)
        m_ref[m] = m_new

    @pl.when(kj == qi)
    def _():
        lv = lam_ref[...]
        lam = (jnp.exp(jnp.sum(lv[0:1] * lv[1:2], axis=1, keepdims=True))
               - jnp.exp(jnp.sum(lv[2:3] * lv[3:4], axis=1, keepdims=True)) + lambda_init)
        o = acc_ref[0] / l_ref[0] - lam * (acc_ref[1] / l_ref[1])
        y = o * lax.rsqrt(jnp.mean(o * o, axis=-1, keepdims=True) + RMS_EPS)
        o_ref[...] = (y * gn_ref[0] * (1.0 - lambda_init)).astype(o_ref.dtype)


def diff_flash(q, k, v, lam_vecs, g_norm, *, batch, seq, lambda_init):
    n = q.shape[0]
    nq = seq // ATT_T
    pairs = [(i, j) for i in range(nq) for j in range(i + 1)]
    qi = jnp.asarray([p[0] for p in pairs], jnp.int32)
    kj = jnp.asarray([p[1] for p in pairs], jnp.int32)
    lam_tab = jnp.zeros((8, 128), F32).at[0:4, 0:DIFF_DH].set(lam_vecs)
    grid_spec = pltpu.PrefetchScalarGridSpec(
        num_scalar_prefetch=2,
        grid=(batch, DIFF_HEADS, len(pairs)),
        in_specs=[
            pl.BlockSpec((8, 128), lambda b, h, s, qi, kj: (0, 0)),
            pl.BlockSpec((ATT_T, 128), lambda b, h, s, qi, kj: (b * nq + qi[s], h)),
            pl.BlockSpec((ATT_T, 128), lambda b, h, s, qi, kj: (b * nq + kj[s], h)),
            pl.BlockSpec((ATT_T, 128), lambda b, h, s, qi, kj: (b * nq + kj[s], h)),
            pl.BlockSpec((1, 1, DIFF_DV), lambda b, h, s, qi, kj: (h, 0, 0)),
        ],
        out_specs=pl.BlockSpec((ATT_T, 128), lambda b, h, s, qi, kj: (b * nq + qi[s], h)),
        scratch_shapes=[pltpu.VMEM((2, ATT_T, 1), F32),
                        pltpu.VMEM((2, ATT_T, 1), F32),
                        pltpu.VMEM((2, ATT_T, DIFF_DV), F32)],
    )
    return pl.pallas_call(
        functools.partial(_flash_kernel, lambda_init=lambda_init),
        grid_spec=grid_spec,
        out_shape=jax.ShapeDtypeStruct((n, DIFF_HEADS * DIFF_DV), BF16),
        compiler_params=_params("parallel", "parallel", "arbitrary"),
        name="diff_flash",
    )(qi, kj, lam_tab, q, k, v, g_norm.reshape(DIFF_HEADS, 1, DIFF_DV))


def _top_values(s, count):
    rows = s.shape[0]
    ridx = lax.broadcasted_iota(jnp.int32, s.shape, 0)
    out = []
    cur = s
    for _ in range(count):
        mx = jnp.max(cur, axis=0, keepdims=True)
        out.append(mx)
        first = jnp.min(jnp.where(cur == mx, ridx, rows), axis=0, keepdims=True)
        cur = jnp.where(ridx == first, -jnp.inf, cur)
    return out


def _peer_select_kernel(q_ref, keys_ref, e1_ref, e2_ref, thr_ref):
    q = q_ref[...].astype(BF16)
    keys = keys_ref[0].astype(BF16)
    s1 = lax.dot_general(keys[0], q[:, :128], NT_DIMS, preferred_element_type=F32)
    s2 = lax.dot_general(keys[1], q[:, 128:], NT_DIMS, preferred_element_type=F32)
    a = _top_values(s1, PEER_TOPK)
    b = _top_values(s2, PEER_TOPK)
    cands = [a[i] + b[j] for i in range(PEER_TOPK) for j in range(PEER_TOPK)
             if (i + 1) * (j + 1) <= PEER_TOPK + 1]
    pad = (-len(cands)) % 8
    cands += [jnp.full_like(a[0], -jnp.inf)] * pad
    best = _top_values(jnp.concatenate(cands, axis=0), PEER_TOPK + 1)
    top = best[0]
    z = best[0] - top
    z = jnp.exp(z)
    for r in range(1, PEER_TOPK):
        z = z + jnp.exp(best[r] - top)
    inv_z = 1.0 / z
    thr_mid = 0.5 * (best[PEER_TOPK - 1] + best[PEER_TOPK])
    e1_ref[0] = jnp.where(s1 >= a[PEER_TOPK - 1], jnp.exp(s1 - a[0]), 0.0)
    e2_ref[0] = jnp.where(s2 >= b[PEER_TOPK - 1], jnp.exp(s2 - b[0]) * inv_z, 0.0)
    thr_ref[0] = jnp.exp(thr_mid - top) * inv_z


def peer_select(qry, sub_keys, *, tt):
    n = qry.shape[0]
    hh = PEER_HEADS
    return pl.pallas_call(
        _peer_select_kernel,
        grid=(n // tt, hh),
        in_specs=[pl.BlockSpec((tt, 256), lambda i, h: (i, h)),
                  pl.BlockSpec((1, 2, PEER_NKEYS, 128), lambda i, h: (h, 0, 0, 0))],
        out_specs=[pl.BlockSpec((1, PEER_NKEYS, tt), lambda i, h: (h, 0, i)),
                   pl.BlockSpec((1, PEER_NKEYS, tt), lambda i, h: (h, 0, i)),
                   pl.BlockSpec((1, 1, tt), lambda i, h: (h, 0, i))],
        out_shape=[jax.ShapeDtypeStruct((hh, PEER_NKEYS, n), F32),
                   jax.ShapeDtypeStruct((hh, PEER_NKEYS, n), F32),
                   jax.ShapeDtypeStruct((hh, 1, n), F32)],
        compiler_params=_params("parallel", "arbitrary"),
        name="peer_select",
    )(qry, sub_keys)


PEER_TT = 512
PEER_ET = 1024


def _peer_dense_kernel(xt_ref, u_ref, vt_ref, e1_ref, e2_ref, thr_ref, o_ref,
                       acts_ref, g_ref):
    @pl.when(pl.program_id(1) == 0)
    def _():
        o_ref[...] = jnp.zeros_like(o_ref)

    acts_ref[...] = jnp.dot(u_ref[...], xt_ref[...], preferred_element_type=F32)

    def block(r, carry):
        w = jnp.zeros((PEER_NKEYS, PEER_TT), F32)
        for h in range(PEER_HEADS):
            p = e1_ref[h, pl.ds(r, 1), :] * e2_ref[h]
            w = w + jnp.where(p >= thr_ref[h], p, 0.0)
        sl = pl.ds(pl.multiple_of(r * PEER_NKEYS, PEER_NKEYS), PEER_NKEYS)
        a = acts_ref[sl, :]
        gelu = 0.5 * a * (1.0 + lax.erf(a * (2.0 ** -0.5)))
        g_ref[sl, :] = (w * gelu).astype(BF16)
        return carry

    lax.fori_loop(0, PEER_ET // PEER_NKEYS, block, 0)
    o_ref[...] += jnp.dot(vt_ref[...], g_ref[...], preferred_element_type=F32)


def peer_dense(xn_t, u, v_t, e1, e2, thr):
    d, n = xn_t.shape
    ne = u.shape[0]
    rows = PEER_ET // PEER_NKEYS
    return pl.pallas_call(
        _peer_dense_kernel,
        grid=(n // PEER_TT, ne // PEER_ET),
        in_specs=[pl.BlockSpec((d, PEER_TT), lambda i, j: (0, i)),
                  pl.BlockSpec((PEER_ET, d), lambda i, j: (j, 0)),
                  pl.BlockSpec((d, PEER_ET), lambda i, j: (0, j)),
                  pl.BlockSpec((PEER_HEADS, rows, PEER_TT), lambda i, j: (0, j, i)),
                  pl.BlockSpec((PEER_HEADS, PEER_NKEYS, PEER_TT), lambda i, j: (0, 0, i)),
                  pl.BlockSpec((PEER_HEADS, 1, PEER_TT), lambda i, j: (0, 0, i))],
        out_specs=pl.BlockSpec((d, PEER_TT), lambda i, j: (0, i)),
        out_shape=jax.ShapeDtypeStruct((d, n), F32),
        scratch_shapes=[pltpu.VMEM((PEER_ET, PEER_TT), F32),
                        pltpu.VMEM((PEER_ET, PEER_TT), BF16)],
        compiler_params=_params("parallel", "arbitrary"),
        name="peer_dense",
    )(xn_t, u, v_t, e1, e2, thr)


def _add_t_kernel(h_ref, ft_ref, g_ref, o_ref, *, final_norm):
    y = h_ref[...] + ft_ref[...].T
    if final_norm:
        y = y * lax.rsqrt(jnp.mean(y * y, axis=-1, keepdims=True) + RMS_EPS) * g_ref[...]
    o_ref[...] = y


def add_transposed(h, ff_t, g, *, tm, final_norm):
    n, d = h.shape
    return pl.pallas_call(
        functools.partial(_add_t_kernel, final_norm=final_norm),
        grid=(n // tm,),
        in_specs=[pl.BlockSpec((tm, d), lambda i: (i, 0)),
                  pl.BlockSpec((d, tm), lambda i: (0, i)),
                  pl.BlockSpec((1, d), lambda i: (0, 0))],
        out_specs=pl.BlockSpec((tm, d), lambda i: (i, 0)),
        out_shape=jax.ShapeDtypeStruct((n, d), F32),
        compiler_params=_params("parallel"),
        name="add_t_norm" if final_norm else "add_t",
    )(h, ff_t, g.reshape(1, d))


def _peer_layer(h, g_ffn, w_query, sub_keys, expert_u, expert_v, g_after, final_norm):
    qry, xn = rms_matmul(h, g_ffn, w_query.astype(BF16), tm=512, tn=1024, emit_xn=True)
    e1, e2, thr = peer_select(qry, sub_keys, tt=512)
    ff_t = peer_dense(xn.T, expert_u.astype(BF16), expert_v.astype(BF16).T, e1, e2, thr)
    return add_transposed(h, ff_t, g_after, tm=512, final_norm=final_norm)


def _gla_layer(h, g_mix, w_in, w_gate_up, b_gate, g_norm, w_out, *, batch, seq):
    pad = GLA_IN_PAD - w_in.shape[1]
    w_in_pad = jnp.pad(w_in, ((0, 0), (0, pad))).astype(BF16)
    w_gate_pad = jnp.pad(w_gate_up, ((0, 128 - GLA_RANK), (0, 0)))
    proj = rms_matmul(h, g_mix, w_in_pad, tm=512, tn=640)
    o = gla_core(proj, w_gate_pad, b_gate, g_norm, batch=batch, seq=seq)
    return matmul_res(o, w_out.astype(BF16), h, tm=512, tn=1024)


def _diff_layer(h, positions, g_mix, w_in, lam_vecs, g_norm, w_out, lambda_init, *, batch, seq):
    proj = rms_matmul(h, g_mix, w_in.astype(BF16), tm=512, tn=1024)
    q, k, v = rope_prep(proj, positions.reshape(-1), tm=512)
    o = diff_flash(q, k, v, lam_vecs, g_norm, batch=batch, seq=seq, lambda_init=lambda_init)
    return matmul_res(o, w_out.astype(BF16), h, tm=512, tn=1024)


def kernel(x, positions, norm_mix, norm_ffn, norm_final, gla_w_in, gla_w_gate_up, gla_b_gate, gla_norm, gla_w_out, diff_w_in, diff_lambda_q1, diff_lambda_k1, diff_lambda_q2, diff_lambda_k2, diff_norm, diff_w_out, peer_w_query, peer_sub_keys, peer_u, peer_v):
    batch, seq, d = x.shape
    depth = norm_mix.shape[0]
    h = x.reshape(batch * seq, d)
    for i in range(depth):
        j = i // 2
        if i % 2 == 0:
            h = _gla_layer(h, norm_mix[i], gla_w_in[j], gla_w_gate_up[j], gla_b_gate[j],
                           gla_norm[j], gla_w_out[j], batch=batch, seq=seq)
        else:
            lambda_init = 0.8 - 0.6 * math.exp(-0.3 * i)
            lam_vecs = jnp.stack([diff_lambda_q1[j], diff_lambda_k1[j],
                                  diff_lambda_q2[j], diff_lambda_k2[j]])
            h = _diff_layer(h, positions, norm_mix[i], diff_w_in[j], lam_vecs, diff_norm[j],
                            diff_w_out[j], lambda_init, batch=batch, seq=seq)
        last = i == depth - 1
        h = _peer_layer(h, norm_ffn[i], peer_w_query[i], peer_sub_keys[i], peer_u[i], peer_v[i],
                        norm_final, final_norm=last)
    return h.reshape(batch, seq, d)
```

```python
import functools
import math

import jax
import jax.numpy as jnp
from jax import lax
from jax.experimental import pallas as pl
from jax.experimental.pallas import tpu as pltpu

D_MODEL = 1024
RMS_EPS = 1e-6

GLA_HEADS = 4
GLA_DK = 128
GLA_DV = 256
GLA_RANK = 16
GLA_TAU = 16.0
GLA_CHUNK = 64
GLA_QK = GLA_HEADS * GLA_DK
GLA_V = GLA_HEADS * GLA_DV
GLA_IN_PAD = 2 * GLA_QK + 2 * GLA_V + 128

DIFF_HEADS = 8
DIFF_DH = 64
DIFF_DV = 128
ROPE_DIMS = 16
ROPE_THETA = 500000.0
LOG2E = 1.4426950408889634
ATT_TQ = 512
ATT_TK = 1024
ATT_ONES = 16
ATT_VROWS = DIFF_DV + ATT_ONES

PEER_HEADS = 8
PEER_NKEYS = 128
PEER_TOPK = 16

V7X_VMEM_LIMIT_BYTES = 48 * 1024 * 1024
BF16 = jnp.bfloat16
F32 = jnp.float32
NT_DIMS = (((1,), (1,)), ((), ()))
TN_DIMS = (((0,), (0,)), ((), ()))


def _params(*sem):
    return pltpu.CompilerParams(dimension_semantics=sem,
                                vmem_limit_bytes=V7X_VMEM_LIMIT_BYTES)


def _rms_matmul_kernel(x_ref, g_ref, w_ref, o_ref, *rest, emit_xn):
    if emit_xn:
        xn_out_ref, xn_ref = rest
    else:
        (xn_ref,) = rest

    @pl.when(pl.program_id(1) == 0)
    def _():
        x = x_ref[...]
        y = x * lax.rsqrt(jnp.mean(x * x, axis=-1, keepdims=True) + RMS_EPS)
        xn_ref[...] = (y * g_ref[...]).astype(BF16)

    if emit_xn:
        xn_out_ref[...] = xn_ref[...]
    o_ref[...] = jnp.dot(xn_ref[...], w_ref[...], preferred_element_type=F32)


def rms_matmul(x, g, w, *, tm, tn, emit_xn=False):
    n, d = x.shape
    m = w.shape[1]
    out_shape = [jax.ShapeDtypeStruct((n, m), F32)]
    out_specs = [pl.BlockSpec((tm, tn), lambda i, j: (i, j))]
    if emit_xn:
        out_shape.append(jax.ShapeDtypeStruct((n, d), BF16))
        out_specs.append(pl.BlockSpec((tm, d), lambda i, j: (i, 0)))
    res = pl.pallas_call(
        functools.partial(_rms_matmul_kernel, emit_xn=emit_xn),
        grid=(n // tm, m // tn),
        in_specs=[pl.BlockSpec((tm, d), lambda i, j: (i, 0)),
                  pl.BlockSpec((1, d), lambda i, j: (0, 0)),
                  pl.BlockSpec((d, tn), lambda i, j: (0, j))],
        out_specs=out_specs,
        out_shape=out_shape,
        scratch_shapes=[pltpu.VMEM((tm, d), BF16)],
        compiler_params=_params("parallel", "arbitrary"),
        name="rms_matmul_xn" if emit_xn else "rms_matmul",
    )(x, g.reshape(1, d), w)
    return res if emit_xn else res[0]


def _matmul_res_kernel(a_ref, w_ref, r_ref, o_ref):
    o_ref[...] = r_ref[...] + jnp.dot(a_ref[...], w_ref[...], preferred_element_type=F32)


def matmul_res(a, w, res, *, tm, tn):
    n, k = a.shape
    m = w.shape[1]
    return pl.pallas_call(
        _matmul_res_kernel,
        grid=(n // tm, m // tn),
        in_specs=[pl.BlockSpec((tm, k), lambda i, j: (i, 0)),
                  pl.BlockSpec((k, tn), lambda i, j: (0, j)),
                  pl.BlockSpec((tm, tn), lambda i, j: (i, j))],
        out_specs=pl.BlockSpec((tm, tn), lambda i, j: (i, j)),
        out_shape=jax.ShapeDtypeStruct((n, m), F32),
        compiler_params=_params("parallel", "arbitrary"),
        name="matmul_res",
    )(a, w, res)


GLA_STEP = 512


def _gla_kernel(q_ref, k_ref, v_ref, r_ref, glr_ref, wg_ref, bg_ref, gn_ref,
                o_ref, state_ref):
    @pl.when(pl.program_id(2) == 0)
    def _():
        state_ref[...] = jnp.zeros_like(state_ref)

    c = GLA_CHUNK
    wg = wg_ref[...].astype(BF16)
    bg = bg_ref[...]
    gn = gn_ref[0]
    row = lax.broadcasted_iota(jnp.int32, (c, c), 0)
    col = lax.broadcasted_iota(jnp.int32, (c, c), 1)
    causal = col <= row
    tril = causal.astype(BF16)

    def chunk(ci, carry):
        sl = pl.ds(pl.multiple_of(ci * c, c), c)
        q = q_ref[sl, :] * (GLA_DK ** -0.5)
        k = k_ref[sl, :]
        v = v_ref[sl, :].astype(BF16)
        z = jnp.dot(glr_ref[sl, :].astype(BF16), wg, preferred_element_type=F32) + bg
        glog = jax.nn.log_sigmoid(z) / GLA_TAU
        g_hi = glog.astype(BF16)
        g_lo = (glog - g_hi.astype(F32)).astype(BF16)
        cum = (jnp.dot(tril, g_hi, preferred_element_type=F32)
               + jnp.dot(tril, g_lo, preferred_element_type=F32))
        last = cum[c - 1:c, :]
        mid = cum[c // 2 - 1:c // 2, :]
        qe = (q * jnp.exp(cum - mid)).astype(BF16)
        ke = (k * jnp.exp(mid - cum)).astype(BF16)
        a = lax.dot_general(qe, ke, NT_DIMS, preferred_element_type=F32)
        a = jnp.where(causal, a, 0.0).astype(BF16)
        o = jnp.dot(a, v, preferred_element_type=F32)
        st = state_ref[...]
        qc = (q * jnp.exp(cum)).astype(BF16)
        o = o + lax.dot_general(qc, st.astype(BF16), NT_DIMS, preferred_element_type=F32)
        kd = (k * jnp.exp(last - cum)).astype(BF16)
        kv_t = lax.dot_general(v, kd, TN_DIMS, preferred_element_type=F32)
        state_ref[...] = st * jnp.exp(last) + kv_t
        y = o * lax.rsqrt(jnp.mean(o * o, axis=-1, keepdims=True) + RMS_EPS) * gn
        r = r_ref[sl, :]
        o_ref[sl, :] = (y * (r * jax.nn.sigmoid(r))).astype(o_ref.dtype)
        return carry

    lax.fori_loop(0, GLA_STEP // c, chunk, 0)


def gla_core(proj, w_gate_up_pad, b_gate, g_norm, *, batch, seq):
    n = proj.shape[0]
    steps = seq // GLA_STEP
    h_ = GLA_HEADS

    def rows(b, h, s):
        return b * steps + s

    return pl.pallas_call(
        _gla_kernel,
        grid=(batch, h_, steps),
        in_specs=[
            pl.BlockSpec((GLA_STEP, GLA_DK), lambda b, h, s: (rows(b, h, s), h)),
            pl.BlockSpec((GLA_STEP, GLA_DK), lambda b, h, s: (rows(b, h, s), h_ + h)),
            pl.BlockSpec((GLA_STEP, GLA_DV), lambda b, h, s: (rows(b, h, s), h_ + h)),
            pl.BlockSpec((GLA_STEP, GLA_DV), lambda b, h, s: (rows(b, h, s), 2 * h_ + h)),
            pl.BlockSpec((GLA_STEP, 128), lambda b, h, s: (rows(b, h, s), (2 * GLA_QK + 2 * GLA_V) // 128)),
            pl.BlockSpec((128, GLA_DK), lambda b, h, s: (0, h)),
            pl.BlockSpec((1, GLA_DK), lambda b, h, s: (0, h)),
            pl.BlockSpec((1, 1, GLA_DV), lambda b, h, s: (h, 0, 0)),
        ],
        out_specs=pl.BlockSpec((GLA_STEP, GLA_DV), lambda b, h, s: (rows(b, h, s), h)),
        out_shape=jax.ShapeDtypeStruct((n, GLA_V), BF16),
        scratch_shapes=[pltpu.VMEM((GLA_DV, GLA_DK), F32)],
        compiler_params=_params("parallel", "parallel", "arbitrary"),
        name="gla_core",
    )(proj, proj, proj, proj, proj, w_gate_up_pad, b_gate.reshape(1, GLA_QK),
      g_norm.reshape(GLA_HEADS, 1, GLA_DV))


def _rope_kernel(pos_ref, freq_ref, q_ref, k_ref, v_ref, qo_ref, ko_ref, vo_ref):
    ang = pos_ref[...].astype(F32) * freq_ref[0:1, :]
    cos = jnp.cos(ang)
    sin = jnp.sin(ang)
    sgn_lo = freq_ref[1:2, :]
    sgn_hi = freq_ref[2:3, :]
    half = ROPE_DIMS // 2

    def rot(t, scale):
        outs = []
        for h in range(DIFF_HEADS):
            x = t[:, h * 128:(h + 1) * 128]
            partner = (sgn_lo * pltpu.roll(x, 128 - half, 1)
                       + sgn_hi * pltpu.roll(x, half, 1))
            outs.append((x * cos + partner * sin) * scale)
        return jnp.concatenate(outs, axis=1)

    qo_ref[...] = rot(q_ref[...], DIFF_DH ** -0.5 * LOG2E).T.astype(BF16)
    ko_ref[...] = rot(k_ref[...], 1.0).astype(BF16)
    vt = v_ref[...].T
    ones = jnp.ones((ATT_ONES, vt.shape[1]), F32)
    vo_ref[...] = jnp.concatenate(
        [piece for h in range(DIFF_HEADS)
         for piece in (vt[h * DIFF_DV:(h + 1) * DIFF_DV], ones)], axis=0).astype(BF16)


def rope_prep(proj, positions, *, tm):
    n = proj.shape[0]
    w = DIFF_HEADS * 128
    lane = jnp.arange(128) % DIFF_DH
    inv_freq = ROPE_THETA ** (-jnp.arange(0, ROPE_DIMS, 2, dtype=F32) / ROPE_DIMS)
    half = ROPE_DIMS // 2
    freq = jnp.where(lane < ROPE_DIMS, inv_freq[lane % half], 0.0)
    sgn_lo = jnp.where(lane < half, -1.0, 0.0)
    sgn_hi = jnp.where((lane >= half) & (lane < ROPE_DIMS), 1.0, 0.0)
    table = jnp.zeros((8, 128), F32).at[0].set(freq).at[1].set(sgn_lo).at[2].set(sgn_hi)
    spec_in = lambda c: pl.BlockSpec((tm, w), lambda i: (i, c))
    row_major = jax.ShapeDtypeStruct((n, w), BF16)
    transposed = jax.ShapeDtypeStruct((w, n), BF16)
    spec_t = pl.BlockSpec((w, tm), lambda i: (0, i))
    wv = DIFF_HEADS * ATT_VROWS
    return pl.pallas_call(
        _rope_kernel,
        grid=(n // tm,),
        in_specs=[pl.BlockSpec((tm, 1), lambda i: (i, 0)),
                  pl.BlockSpec((8, 128), lambda i: (0, 0)),
                  spec_in(0), spec_in(1), spec_in(2)],
        out_specs=[spec_t, pl.BlockSpec((tm, w), lambda i: (i, 0)),
                   pl.BlockSpec((wv, tm), lambda i: (0, i))],
        out_shape=[transposed, row_major, jax.ShapeDtypeStruct((wv, n), BF16)],
        compiler_params=_params("parallel"),
        name="rope_prep",
    )(positions.reshape(n, 1), table, proj, proj, proj)


def _flash_kernel(qi_ref, kj_ref, lam_ref, qt_ref, k_ref, vt_ref, gn_ref, o_ref,
                  m_ref, acc_ref, s_ref, *, lambda_init):
    step = pl.program_id(2)
    qi = qi_ref[step]
    kj = kj_ref[step]
    last_kj = (qi * ATT_TQ) // ATT_TK

    @pl.when(kj == 0)
    def _():
        m_ref[...] = jnp.full_like(m_ref, -jnp.inf)
        acc_ref[...] = jnp.zeros_like(acc_ref)

    def accumulate(masked):
        qt = qt_ref[...]
        k = k_ref[...]
        vt = vt_ref[...]
        lane = lax.broadcasted_iota(jnp.int32, k.shape, 1)
        if masked:
            key = lax.broadcasted_iota(jnp.int32, (ATT_TK, ATT_TQ), 0)
            qry = lax.broadcasted_iota(jnp.int32, (ATT_TK, ATT_TQ), 1)
            visible = key <= qry + (qi * ATT_TQ - kj * ATT_TK)
        for m in range(2):
            in_map = (lane < DIFF_DH) if m == 0 else (lane >= DIFF_DH)
            km = jnp.where(in_map, k, jnp.zeros_like(k))
            s_ref[m] = jnp.dot(km, qt, preferred_element_type=F32)
        for m in range(2):
            s = s_ref[m]
            if masked:
                s = jnp.where(visible, s, -jnp.inf)
            m_old = m_ref[m]
            m_new = jnp.maximum(m_old, jnp.max(s, axis=0, keepdims=True))
            p = jnp.exp2((s - m_new).astype(BF16))
            alpha = jnp.exp2(m_old - m_new)
            acc_ref[m] = alpha * acc_ref[m] + jnp.dot(vt, p, preferred_element_type=F32)
            m_ref[m] = m_new

    @pl.when(kj < last_kj)
    def _():
        accumulate(False)

    @pl.when(kj == last_kj)
    def _():
        accumulate(True)
        lv = lam_ref[...]
        lam = (jnp.exp(jnp.sum(lv[0:1] * lv[1:2], axis=1, keepdims=True))
               - jnp.exp(jnp.sum(lv[2:3] * lv[3:4], axis=1, keepdims=True)) + lambda_init)
        a0 = acc_ref[0]
        a1 = acc_ref[1]
        o = (a0[:DIFF_DV] / a0[DIFF_DV:DIFF_DV + 1]
             - lam * (a1[:DIFF_DV] / a1[DIFF_DV:DIFF_DV + 1]))
        y = o * lax.rsqrt(jnp.mean(o * o, axis=0, keepdims=True) + RMS_EPS)
        y = y * (gn_ref[0] * (1.0 - lambda_init))
        o_ref[...] = y.T.astype(o_ref.dtype)


def diff_flash(qt, k, vt, lam_vecs, g_norm, *, batch, seq, lambda_init):
    n = k.shape[0]
    nq = seq // ATT_TQ
    nk = seq // ATT_TK
    pairs = [(i, j) for i in range(nq) for j in range((i * ATT_TQ) // ATT_TK + 1)]
    qi = jnp.asarray([p[0] for p in pairs], jnp.int32)
    kj = jnp.asarray([p[1] for p in pairs], jnp.int32)
    lam_tab = jnp.zeros((8, 128), F32).at[0:4, 0:DIFF_DH].set(lam_vecs)
    grid_spec = pltpu.PrefetchScalarGridSpec(
        num_scalar_prefetch=2,
        grid=(batch, DIFF_HEADS, len(pairs)),
        in_specs=[
            pl.BlockSpec((8, 128), lambda b, h, s, qi, kj: (0, 0)),
            pl.BlockSpec((128, ATT_TQ), lambda b, h, s, qi, kj: (h, b * nq + qi[s])),
            pl.BlockSpec((ATT_TK, 128), lambda b, h, s, qi, kj: (b * nk + kj[s], h)),
            pl.BlockSpec((ATT_VROWS, ATT_TK), lambda b, h, s, qi, kj: (h, b * nk + kj[s])),
            pl.BlockSpec((1, DIFF_DV, 1), lambda b, h, s, qi, kj: (h, 0, 0)),
        ],
        out_specs=pl.BlockSpec((ATT_TQ, 128), lambda b, h, s, qi, kj: (b * nq + qi[s], h)),
        scratch_shapes=[pltpu.VMEM((2, 1, ATT_TQ), F32),
                        pltpu.VMEM((2, ATT_VROWS, ATT_TQ), F32),
                        pltpu.VMEM((2, ATT_TK, ATT_TQ), F32)],
    )
    return pl.pallas_call(
        functools.partial(_flash_kernel, lambda_init=lambda_init),
        grid_spec=grid_spec,
        out_shape=jax.ShapeDtypeStruct((n, DIFF_HEADS * DIFF_DV), BF16),
        compiler_params=_params("parallel", "parallel", "arbitrary"),
        name="diff_flash",
    )(qi, kj, lam_tab, qt, k, vt, g_norm.reshape(DIFF_HEADS, DIFF_DV, 1))


def _top_values(s, count):
    rows = s.shape[0]
    ridx = lax.broadcasted_iota(jnp.int32, s.shape, 0)
    out = []
    cur = s
    for _ in range(count):
        mx = jnp.max(cur, axis=0, keepdims=True)
        out.append(mx)
        first = jnp.min(jnp.where(cur == mx, ridx, rows), axis=0, keepdims=True)
        cur = jnp.where(ridx == first, -jnp.inf, cur)
    return out


def _peer_select_kernel(q_ref, keys_ref, e1_ref, e2_ref, thr_ref):
    q = q_ref[...].astype(BF16)
    keys = keys_ref[0].astype(BF16)
    s1 = lax.dot_general(keys[0], q[:, :128], NT_DIMS, preferred_element_type=F32)
    s2 = lax.dot_general(keys[1], q[:, 128:], NT_DIMS, preferred_element_type=F32)
    a = _top_values(s1, PEER_TOPK)
    b = _top_values(s2, PEER_TOPK)
    cands = [a[i] + b[j] for i in range(PEER_TOPK) for j in range(PEER_TOPK)
             if (i + 1) * (j + 1) <= PEER_TOPK + 1]
    pad = (-len(cands)) % 8
    cands += [jnp.full_like(a[0], -jnp.inf)] * pad
    best = _top_values(jnp.concatenate(cands, axis=0), PEER_TOPK + 1)
    top = best[0]
    z = best[0] - top
    z = jnp.exp(z)
    for r in range(1, PEER_TOPK):
        z = z + jnp.exp(best[r] - top)
    inv_z = 1.0 / z
    thr_mid = 0.5 * (best[PEER_TOPK - 1] + best[PEER_TOPK])
    e1_ref[0] = jnp.where(s1 >= a[PEER_TOPK - 1], jnp.exp(s1 - a[0]), 0.0)
    e2_ref[0] = jnp.where(s2 >= b[PEER_TOPK - 1], jnp.exp(s2 - b[0]) * inv_z, 0.0)
    thr_ref[0] = jnp.exp(thr_mid - top) * inv_z


def peer_select(qry, sub_keys, *, tt):
    n = qry.shape[0]
    hh = PEER_HEADS
    return pl.pallas_call(
        _peer_select_kernel,
        grid=(n // tt, hh),
        in_specs=[pl.BlockSpec((tt, 256), lambda i, h: (i, h)),
                  pl.BlockSpec((1, 2, PEER_NKEYS, 128), lambda i, h: (h, 0, 0, 0))],
        out_specs=[pl.BlockSpec((1, PEER_NKEYS, tt), lambda i, h: (h, 0, i)),
                   pl.BlockSpec((1, PEER_NKEYS, tt), lambda i, h: (h, 0, i)),
                   pl.BlockSpec((1, 1, tt), lambda i, h: (h, 0, i))],
        out_shape=[jax.ShapeDtypeStruct((hh, PEER_NKEYS, n), F32),
                   jax.ShapeDtypeStruct((hh, PEER_NKEYS, n), F32),
                   jax.ShapeDtypeStruct((hh, 1, n), F32)],
        compiler_params=_params("parallel", "arbitrary"),
        name="peer_select",
    )(qry, sub_keys)


PEER_TT = 512
PEER_ET = 1024


def _peer_dense_kernel(xt_ref, u_ref, vt_ref, e1_ref, e2_ref, thr_ref, o_ref,
                       acts_ref, g_ref):
    @pl.when(pl.program_id(1) == 0)
    def _():
        o_ref[...] = jnp.zeros_like(o_ref)

    acts_ref[...] = jnp.dot(u_ref[...], xt_ref[...], preferred_element_type=F32)

    def block(r, carry):
        w = jnp.zeros((PEER_NKEYS, PEER_TT), F32)
        for h in range(PEER_HEADS):
            p = e1_ref[h, pl.ds(r, 1), :] * e2_ref[h]
            w = w + jnp.where(p >= thr_ref[h], p, 0.0)
        sl = pl.ds(pl.multiple_of(r * PEER_NKEYS, PEER_NKEYS), PEER_NKEYS)
        a = acts_ref[sl, :]
        gelu = 0.5 * a * (1.0 + lax.erf(a * (2.0 ** -0.5)))
        g_ref[sl, :] = (w * gelu).astype(BF16)
        return carry

    lax.fori_loop(0, PEER_ET // PEER_NKEYS, block, 0)
    o_ref[...] += jnp.dot(vt_ref[...], g_ref[...], preferred_element_type=F32)


def peer_dense(xn_t, u, v_t, e1, e2, thr):
    d, n = xn_t.shape
    ne = u.shape[0]
    rows = PEER_ET // PEER_NKEYS
    return pl.pallas_call(
        _peer_dense_kernel,
        grid=(n // PEER_TT, ne // PEER_ET),
        in_specs=[pl.BlockSpec((d, PEER_TT), lambda i, j: (0, i)),
                  pl.BlockSpec((PEER_ET, d), lambda i, j: (j, 0)),
                  pl.BlockSpec((d, PEER_ET), lambda i, j: (0, j)),
                  pl.BlockSpec((PEER_HEADS, rows, PEER_TT), lambda i, j: (0, j, i)),
                  pl.BlockSpec((PEER_HEADS, PEER_NKEYS, PEER_TT), lambda i, j: (0, 0, i)),
                  pl.BlockSpec((PEER_HEADS, 1, PEER_TT), lambda i, j: (0, 0, i))],
        out_specs=pl.BlockSpec((d, PEER_TT), lambda i, j: (0, i)),
        out_shape=jax.ShapeDtypeStruct((d, n), F32),
        scratch_shapes=[pltpu.VMEM((PEER_ET, PEER_TT), F32),
                        pltpu.VMEM((PEER_ET, PEER_TT), BF16)],
        compiler_params=_params("parallel", "arbitrary"),
        name="peer_dense",
    )(xn_t, u, v_t, e1, e2, thr)


def _add_t_kernel(h_ref, ft_ref, g_ref, o_ref, *, final_norm):
    y = h_ref[...] + ft_ref[...].T
    if final_norm:
        y = y * lax.rsqrt(jnp.mean(y * y, axis=-1, keepdims=True) + RMS_EPS) * g_ref[...]
    o_ref[...] = y


def add_transposed(h, ff_t, g, *, tm, final_norm):
    n, d = h.shape
    return pl.pallas_call(
        functools.partial(_add_t_kernel, final_norm=final_norm),
        grid=(n // tm,),
        in_specs=[pl.BlockSpec((tm, d), lambda i: (i, 0)),
                  pl.BlockSpec((d, tm), lambda i: (0, i)),
                  pl.BlockSpec((1, d), lambda i: (0, 0))],
        out_specs=pl.BlockSpec((tm, d), lambda i: (i, 0)),
        out_shape=jax.ShapeDtypeStruct((n, d), F32),
        compiler_params=_params("parallel"),
        name="add_t_norm" if final_norm else "add_t",
    )(h, ff_t, g.reshape(1, d))


def _peer_layer(h, g_ffn, w_query, sub_keys, expert_u, expert_v, g_after, final_norm):
    qry, xn = rms_matmul(h, g_ffn, w_query.astype(BF16), tm=512, tn=1024, emit_xn=True)
    e1, e2, thr = peer_select(qry, sub_keys, tt=512)
    ff_t = peer_dense(xn.T, expert_u.astype(BF16), expert_v.astype(BF16).T, e1, e2, thr)
    return add_transposed(h, ff_t, g_after, tm=512, final_norm=final_norm)


def _gla_layer(h, g_mix, w_in, w_gate_up, b_gate, g_norm, w_out, *, batch, seq):
    pad = GLA_IN_PAD - w_in.shape[1]
    w_in_pad = jnp.pad(w_in, ((0, 0), (0, pad))).astype(BF16)
    w_gate_pad = jnp.pad(w_gate_up, ((0, 128 - GLA_RANK), (0, 0)))
    proj = rms_matmul(h, g_mix, w_in_pad, tm=512, tn=640)
    o = gla_core(proj, w_gate_pad, b_gate, g_norm, batch=batch, seq=seq)
    return matmul_res(o, w_out.astype(BF16), h, tm=512, tn=1024)


def _diff_layer(h, positions, g_mix, w_in, lam_vecs, g_norm, w_out, lambda_init, *, batch, seq):
    proj = rms_matmul(h, g_mix, w_in.astype(BF16), tm=512, tn=1024)
    qt, k, vt = rope_prep(proj, positions.reshape(-1), tm=512)
    o = diff_flash(qt, k, vt, lam_vecs, g_norm, batch=batch, seq=seq, lambda_init=lambda_init)
    return matmul_res(o, w_out.astype(BF16), h, tm=512, tn=1024)


def kernel(x, positions, norm_mix, norm_ffn, norm_final, gla_w_in, gla_w_gate_up, gla_b_gate, gla_norm, gla_w_out, diff_w_in, diff_lambda_q1, diff_lambda_k1, diff_lambda_q2, diff_lambda_k2, diff_norm, diff_w_out, peer_w_query, peer_sub_keys, peer_u, peer_v):
    batch, seq, d = x.shape
    depth = norm_mix.shape[0]
    h = x.reshape(batch * seq, d)
    for i in range(depth):
        j = i // 2
        if i % 2 == 0:
            h = _gla_layer(h, norm_mix[i], gla_w_in[j], gla_w_gate_up[j], gla_b_gate[j],
                           gla_norm[j], gla_w_out[j], batch=batch, seq=seq)
        else:
            lambda_init = 0.8 - 0.6 * math.exp(-0.3 * i)
            lam_vecs = jnp.stack([diff_lambda_q1[j], diff_lambda_k1[j],
                                  diff_lambda_q2[j], diff_lambda_k2[j]])
            h = _diff_layer(h, positions, norm_mix[i], diff_w_in[j], lam_vecs, diff_norm[j],
                            diff_w_out[j], lambda_init, batch=batch, seq=seq)
        last = i == depth - 1
        h = _peer_layer(h, norm_ffn[i], peer_w_query[i], peer_sub_keys[i], peer_u[i], peer_v[i],
                        norm_final, final_norm=last)
    return h.reshape(batch, seq, d)
```

```python
import functools
import math

import jax
import jax.numpy as jnp
from jax import lax
from jax.experimental import pallas as pl
from jax.experimental.pallas import tpu as pltpu

D_MODEL = 1024
RMS_EPS = 1e-6

GLA_HEADS = 4
GLA_DK = 128
GLA_DV = 256
GLA_RANK = 16
GLA_TAU = 16.0
GLA_CHUNK = 64
GLA_QK = GLA_HEADS * GLA_DK
GLA_V = GLA_HEADS * GLA_DV
GLA_IN_PAD = 2 * GLA_QK + 2 * GLA_V + 128

DIFF_HEADS = 8
DIFF_DH = 64
DIFF_DV = 128
ROPE_DIMS = 16
ROPE_THETA = 500000.0
LOG2E = 1.4426950408889634
ATT_TQ = 1024
ATT_TK = 1024
ATT_ONES = 16
ATT_VROWS = DIFF_DV + ATT_ONES
ATT_SUB = 256

PEER_HEADS = 8
PEER_NKEYS = 128
PEER_TOPK = 16

V7X_VMEM_LIMIT_BYTES = 48 * 1024 * 1024
BF16 = jnp.bfloat16
F32 = jnp.float32
NT_DIMS = (((1,), (1,)), ((), ()))
TN_DIMS = (((0,), (0,)), ((), ()))


def _params(*sem):
    return pltpu.CompilerParams(dimension_semantics=sem,
                                vmem_limit_bytes=V7X_VMEM_LIMIT_BYTES)


def _rms_matmul_kernel(x_ref, g_ref, w_ref, o_ref, *rest, emit_xn):
    if emit_xn:
        xn_out_ref, xn_ref = rest
    else:
        (xn_ref,) = rest

    @pl.when(pl.program_id(1) == 0)
    def _():
        x = x_ref[...]
        y = x * lax.rsqrt(jnp.mean(x * x, axis=-1, keepdims=True) + RMS_EPS)
        xn_ref[...] = (y * g_ref[...]).astype(BF16)

    if emit_xn:
        xn_out_ref[...] = xn_ref[...]
    o_ref[...] = jnp.dot(xn_ref[...], w_ref[...], preferred_element_type=F32)


def rms_matmul(x, g, w, *, tm, tn, emit_xn=False):
    n, d = x.shape
    m = w.shape[1]
    out_shape = [jax.ShapeDtypeStruct((n, m), F32)]
    out_specs = [pl.BlockSpec((tm, tn), lambda i, j: (i, j))]
    if emit_xn:
        out_shape.append(jax.ShapeDtypeStruct((n, d), BF16))
        out_specs.append(pl.BlockSpec((tm, d), lambda i, j: (i, 0)))
    res = pl.pallas_call(
        functools.partial(_rms_matmul_kernel, emit_xn=emit_xn),
        grid=(n // tm, m // tn),
        in_specs=[pl.BlockSpec((tm, d), lambda i, j: (i, 0)),
                  pl.BlockSpec((1, d), lambda i, j: (0, 0)),
                  pl.BlockSpec((d, tn), lambda i, j: (0, j))],
        out_specs=out_specs,
        out_shape=out_shape,
        scratch_shapes=[pltpu.VMEM((tm, d), BF16)],
        compiler_params=_params("parallel", "arbitrary"),
        name="rms_matmul_xn" if emit_xn else "rms_matmul",
    )(x, g.reshape(1, d), w)
    return res if emit_xn else res[0]


def _matmul_res_kernel(a_ref, w_ref, r_ref, o_ref):
    o_ref[...] = r_ref[...] + jnp.dot(a_ref[...], w_ref[...], preferred_element_type=F32)


def matmul_res(a, w, res, *, tm, tn):
    n, k = a.shape
    m = w.shape[1]
    return pl.pallas_call(
        _matmul_res_kernel,
        grid=(n // tm, m // tn),
        in_specs=[pl.BlockSpec((tm, k), lambda i, j: (i, 0)),
                  pl.BlockSpec((k, tn), lambda i, j: (0, j)),
                  pl.BlockSpec((tm, tn), lambda i, j: (i, j))],
        out_specs=pl.BlockSpec((tm, tn), lambda i, j: (i, j)),
        out_shape=jax.ShapeDtypeStruct((n, m), F32),
        compiler_params=_params("parallel", "arbitrary"),
        name="matmul_res",
    )(a, w, res)


GLA_STEP = 512


def _gla_kernel(q_ref, k_ref, v_ref, r_ref, glr_ref, wg_ref, bg_ref, gn_ref,
                o_ref, state_ref):
    @pl.when(pl.program_id(1) == 0)
    def _():
        state_ref[...] = jnp.zeros_like(state_ref)

    c = GLA_CHUNK
    wg = wg_ref[...].astype(BF16)
    bg = bg_ref[...]
    row = lax.broadcasted_iota(jnp.int32, (c, c), 0)
    col = lax.broadcasted_iota(jnp.int32, (c, c), 1)
    causal = col <= row
    tril = causal.astype(BF16)

    def chunk(ci, carry):
        sl = pl.ds(pl.multiple_of(ci * c, c), c)
        q_all = q_ref[sl, :] * (GLA_DK ** -0.5)
        k_all = k_ref[sl, :]
        v_all = v_ref[sl, :].astype(BF16)
        r_all = r_ref[sl, :]
        z = jnp.dot(glr_ref[sl, :].astype(BF16), wg, preferred_element_type=F32) + bg
        glog = jax.nn.log_sigmoid(z) / GLA_TAU
        g_hi = glog.astype(BF16)
        g_lo = (glog - g_hi.astype(F32)).astype(BF16)
        cum_all = (jnp.dot(tril, g_hi, preferred_element_type=F32)
                   + jnp.dot(tril, g_lo, preferred_element_type=F32))
        outs = []
        for h in range(GLA_HEADS):
            ks = slice(h * GLA_DK, (h + 1) * GLA_DK)
            vs = slice(h * GLA_DV, (h + 1) * GLA_DV)
            q, k, v, cum = q_all[:, ks], k_all[:, ks], v_all[:, vs], cum_all[:, ks]
            last = cum[c - 1:c, :]
            mid = cum[c // 2 - 1:c // 2, :]
            qe = (q * jnp.exp(cum - mid)).astype(BF16)
            ke = (k * jnp.exp(mid - cum)).astype(BF16)
            a = lax.dot_general(qe, ke, NT_DIMS, preferred_element_type=F32)
            a = jnp.where(causal, a, 0.0).astype(BF16)
            o = jnp.dot(a, v, preferred_element_type=F32)
            st = state_ref[h]
            qc = (q * jnp.exp(cum)).astype(BF16)
            o = o + lax.dot_general(qc, st.astype(BF16), NT_DIMS, preferred_element_type=F32)
            kd = (k * jnp.exp(last - cum)).astype(BF16)
            kv_t = lax.dot_general(v, kd, TN_DIMS, preferred_element_type=F32)
            state_ref[h] = st * jnp.exp(last) + kv_t
            y = o * lax.rsqrt(jnp.mean(o * o, axis=-1, keepdims=True) + RMS_EPS) * gn_ref[h]
            r = r_all[:, vs]
            outs.append((y * (r * jax.nn.sigmoid(r))).astype(o_ref.dtype))
        o_ref[sl, :] = jnp.concatenate(outs, axis=1)
        return carry

    lax.fori_loop(0, GLA_STEP // c, chunk, 0)


def gla_core(proj, w_gate_up_pad, b_gate, g_norm, *, batch, seq):
    n = proj.shape[0]
    steps = seq // GLA_STEP

    def rows(b, s):
        return b * steps + s

    return pl.pallas_call(
        _gla_kernel,
        grid=(batch, steps),
        in_specs=[
            pl.BlockSpec((GLA_STEP, GLA_QK), lambda b, s: (rows(b, s), 0)),
            pl.BlockSpec((GLA_STEP, GLA_QK), lambda b, s: (rows(b, s), 1)),
            pl.BlockSpec((GLA_STEP, GLA_V), lambda b, s: (rows(b, s), 1)),
            pl.BlockSpec((GLA_STEP, GLA_V), lambda b, s: (rows(b, s), 2)),
            pl.BlockSpec((GLA_STEP, 128), lambda b, s: (rows(b, s), (2 * GLA_QK + 2 * GLA_V) // 128)),
            pl.BlockSpec((128, GLA_QK), lambda b, s: (0, 0)),
            pl.BlockSpec((1, GLA_QK), lambda b, s: (0, 0)),
            pl.BlockSpec((GLA_HEADS, 1, GLA_DV), lambda b, s: (0, 0, 0)),
        ],
        out_specs=pl.BlockSpec((GLA_STEP, GLA_V), lambda b, s: (rows(b, s), 0)),
        out_shape=jax.ShapeDtypeStruct((n, GLA_V), BF16),
        scratch_shapes=[pltpu.VMEM((GLA_HEADS, GLA_DV, GLA_DK), F32)],
        compiler_params=_params("parallel", "arbitrary"),
        name="gla_core",
    )(proj, proj, proj, proj, proj, w_gate_up_pad, b_gate.reshape(1, GLA_QK),
      g_norm.reshape(GLA_HEADS, 1, GLA_DV))


def _rope_kernel(pos_ref, freq_ref, q_ref, k_ref, v_ref, qo_ref, ko_ref, vo_ref):
    ang = pos_ref[...].astype(F32) * freq_ref[0:1, :]
    cos = jnp.cos(ang)
    sin = jnp.sin(ang)
    sgn_lo = freq_ref[1:2, :]
    sgn_hi = freq_ref[2:3, :]
    half = ROPE_DIMS // 2

    def rot(t, scale):
        outs = []
        for h in range(DIFF_HEADS):
            x = t[:, h * 128:(h + 1) * 128]
            partner = (sgn_lo * pltpu.roll(x, 128 - half, 1)
                       + sgn_hi * pltpu.roll(x, half, 1))
            outs.append((x * cos + partner * sin) * scale)
        return jnp.concatenate(outs, axis=1)

    qo_ref[...] = rot(q_ref[...], DIFF_DH ** -0.5 * LOG2E).T.astype(BF16)
    ko_ref[...] = rot(k_ref[...], 1.0).astype(BF16)
    vt = v_ref[...].T
    ones = jnp.ones((ATT_ONES, vt.shape[1]), F32)
    vo_ref[...] = jnp.concatenate(
        [piece for h in range(DIFF_HEADS)
         for piece in (vt[h * DIFF_DV:(h + 1) * DIFF_DV], ones)], axis=0).astype(BF16)


def rope_prep(proj, positions, *, tm):
    n = proj.shape[0]
    w = DIFF_HEADS * 128
    lane = jnp.arange(128) % DIFF_DH
    inv_freq = ROPE_THETA ** (-jnp.arange(0, ROPE_DIMS, 2, dtype=F32) / ROPE_DIMS)
    half = ROPE_DIMS // 2
    freq = jnp.where(lane < ROPE_DIMS, inv_freq[lane % half], 0.0)
    sgn_lo = jnp.where(lane < half, -1.0, 0.0)
    sgn_hi = jnp.where((lane >= half) & (lane < ROPE_DIMS), 1.0, 0.0)
    table = jnp.zeros((8, 128), F32).at[0].set(freq).at[1].set(sgn_lo).at[2].set(sgn_hi)
    spec_in = lambda c: pl.BlockSpec((tm, w), lambda i: (i, c))
    row_major = jax.ShapeDtypeStruct((n, w), BF16)
    transposed = jax.ShapeDtypeStruct((w, n), BF16)
    spec_t = pl.BlockSpec((w, tm), lambda i: (0, i))
    wv = DIFF_HEADS * ATT_VROWS
    return pl.pallas_call(
        _rope_kernel,
        grid=(n // tm,),
        in_specs=[pl.BlockSpec((tm, 1), lambda i: (i, 0)),
                  pl.BlockSpec((8, 128), lambda i: (0, 0)),
                  spec_in(0), spec_in(1), spec_in(2)],
        out_specs=[spec_t, pl.BlockSpec((tm, w), lambda i: (i, 0)),
                   pl.BlockSpec((wv, tm), lambda i: (0, i))],
        out_shape=[transposed, row_major, jax.ShapeDtypeStruct((wv, n), BF16)],
        compiler_params=_params("parallel"),
        name="rope_prep",
    )(positions.reshape(n, 1), table, proj, proj, proj)


def _flash_kernel(qi_ref, kj_ref, lam_ref, qt_ref, k_ref, vt_ref, gn_ref, o_ref,
                  m_ref, acc_ref, s_ref, *, lambda_init):
    step = pl.program_id(2)
    qi = qi_ref[step]
    kj = kj_ref[step]
    last_kj = (qi * ATT_TQ) // ATT_TK

    @pl.when(kj == 0)
    def _():
        m_ref[...] = jnp.full_like(m_ref, -jnp.inf)
        acc_ref[...] = jnp.zeros_like(acc_ref)

    def accumulate(masked):
        qt = qt_ref[...]
        lane = lax.broadcasted_iota(jnp.int32, (ATT_SUB, 128), 1)
        if masked:
            key = lax.broadcasted_iota(jnp.int32, (ATT_SUB, ATT_TQ), 0)
            qry = lax.broadcasted_iota(jnp.int32, (ATT_SUB, ATT_TQ), 1)
            visible0 = key <= qry + (qi * ATT_TQ - kj * ATT_TK)

        def scores(c):
            k = k_ref[c * ATT_SUB:(c + 1) * ATT_SUB, :]
            for m in range(2):
                in_map = (lane < DIFF_DH) if m == 0 else (lane >= DIFF_DH)
                km = jnp.where(in_map, k, jnp.zeros_like(k))
                s_ref[m, c] = jnp.dot(km, qt, preferred_element_type=F32)

        def softmax_pv(c):
            vt = vt_ref[:, c * ATT_SUB:(c + 1) * ATT_SUB]
            for m in range(2):
                s = s_ref[m, c]
                if masked:
                    visible = visible0 if c == 0 else (key + c * ATT_SUB
                                                       <= qry + (qi * ATT_TQ - kj * ATT_TK))
                    s = jnp.where(visible, s, -jnp.inf)
                m_old = m_ref[m]
                m_new = jnp.maximum(m_old, jnp.max(s, axis=0, keepdims=True))
                p = jnp.exp2((s - m_new).astype(BF16))
                alpha = jnp.exp2(m_old - m_new)
                acc_ref[m] = alpha * acc_ref[m] + jnp.dot(vt, p, preferred_element_type=F32)
                m_ref[m] = m_new

        n_sub = ATT_TK // ATT_SUB
        scores(0)
        for c in range(n_sub):
            if c + 1 < n_sub:
                scores(c + 1)
            softmax_pv(c)

    @pl.when(kj < last_kj)
    def _():
        accumulate(False)

    @pl.when(kj == last_kj)
    def _():
        accumulate(True)
        lv = lam_ref[...]
        lam = (jnp.exp(jnp.sum(lv[0:1] * lv[1:2], axis=1, keepdims=True))
               - jnp.exp(jnp.sum(lv[2:3] * lv[3:4], axis=1, keepdims=True)) + lambda_init)
        a0 = acc_ref[0]
        a1 = acc_ref[1]
        o = (a0[:DIFF_DV] / a0[DIFF_DV:DIFF_DV + 1]
             - lam * (a1[:DIFF_DV] / a1[DIFF_DV:DIFF_DV + 1]))
        y = o * lax.rsqrt(jnp.mean(o * o, axis=0, keepdims=True) + RMS_EPS)
        y = y * (gn_ref[0] * (1.0 - lambda_init))
        o_ref[...] = y.T.astype(o_ref.dtype)


def diff_flash(qt, k, vt, lam_vecs, g_norm, *, batch, seq, lambda_init):
    n = k.shape[0]
    nq = seq // ATT_TQ
    nk = seq // ATT_TK
    pairs = [(i, j) for i in range(nq) for j in range((i * ATT_TQ) // ATT_TK + 1)]
    qi = jnp.asarray([p[0] for p in pairs], jnp.int32)
    kj = jnp.asarray([p[1] for p in pairs], jnp.int32)
    lam_tab = jnp.zeros((8, 128), F32).at[0:4, 0:DIFF_DH].set(lam_vecs)
    grid_spec = pltpu.PrefetchScalarGridSpec(
        num_scalar_prefetch=2,
        grid=(batch, DIFF_HEADS, len(pairs)),
        in_specs=[
            pl.BlockSpec((8, 128), lambda b, h, s, qi, kj: (0, 0)),
            pl.BlockSpec((128, ATT_TQ), lambda b, h, s, qi, kj: (h, b * nq + qi[s])),
            pl.BlockSpec((ATT_TK, 128), lambda b, h, s, qi, kj: (b * nk + kj[s], h)),
            pl.BlockSpec((ATT_VROWS, ATT_TK), lambda b, h, s, qi, kj: (h, b * nk + kj[s])),
            pl.BlockSpec((1, DIFF_DV, 1), lambda b, h, s, qi, kj: (h, 0, 0)),
        ],
        out_specs=pl.BlockSpec((ATT_TQ, 128), lambda b, h, s, qi, kj: (b * nq + qi[s], h)),
        scratch_shapes=[pltpu.VMEM((2, 1, ATT_TQ), F32),
                        pltpu.VMEM((2, ATT_VROWS, ATT_TQ), F32),
                        pltpu.VMEM((2, ATT_TK // ATT_SUB, ATT_SUB, ATT_TQ), F32)],
    )
    return pl.pallas_call(
        functools.partial(_flash_kernel, lambda_init=lambda_init),
        grid_spec=grid_spec,
        out_shape=jax.ShapeDtypeStruct((n, DIFF_HEADS * DIFF_DV), BF16),
        compiler_params=_params("parallel", "parallel", "arbitrary"),
        name="diff_flash",
    )(qi, kj, lam_tab, qt, k, vt, g_norm.reshape(DIFF_HEADS, DIFF_DV, 1))


def _top_values(s, count):
    rows = s.shape[0]
    ridx = lax.broadcasted_iota(jnp.int32, s.shape, 0)
    out = []
    cur = s
    for _ in range(count):
        mx = jnp.max(cur, axis=0, keepdims=True)
        out.append(mx)
        first = jnp.min(jnp.where(cur == mx, ridx, rows), axis=0, keepdims=True)
        cur = jnp.where(ridx == first, -jnp.inf, cur)
    return out


def _peer_select_kernel(q_ref, keys_ref, e1_ref, e2_ref, thr_ref):
    q = q_ref[...].astype(BF16)
    keys = keys_ref[0].astype(BF16)
    s1 = lax.dot_general(keys[0], q[:, :128], NT_DIMS, preferred_element_type=F32)
    s2 = lax.dot_general(keys[1], q[:, 128:], NT_DIMS, preferred_element_type=F32)
    a = _top_values(s1, PEER_TOPK)
    b = _top_values(s2, PEER_TOPK)
    cands = [a[i] + b[j] for i in range(PEER_TOPK) for j in range(PEER_TOPK)
             if (i + 1) * (j + 1) <= PEER_TOPK + 1]
    pad = (-len(cands)) % 8
    cands += [jnp.full_like(a[0], -jnp.inf)] * pad
    best = _top_values(jnp.concatenate(cands, axis=0), PEER_TOPK + 1)
    top = best[0]
    z = best[0] - top
    z = jnp.exp(z)
    for r in range(1, PEER_TOPK):
        z = z + jnp.exp(best[r] - top)
    inv_z = 1.0 / z
    thr_mid = 0.5 * (best[PEER_TOPK - 1] + best[PEER_TOPK])
    e1_ref[0] = jnp.where(s1 >= a[PEER_TOPK - 1], jnp.exp(s1 - a[0]), 0.0)
    e2_ref[0] = jnp.where(s2 >= b[PEER_TOPK - 1], jnp.exp(s2 - b[0]) * inv_z, 0.0)
    thr_ref[0] = jnp.exp(thr_mid - top) * inv_z


def peer_select(qry, sub_keys, *, tt):
    n = qry.shape[0]
    hh = PEER_HEADS
    return pl.pallas_call(
        _peer_select_kernel,
        grid=(n // tt, hh),
        in_specs=[pl.BlockSpec((tt, 256), lambda i, h: (i, h)),
                  pl.BlockSpec((1, 2, PEER_NKEYS, 128), lambda i, h: (h, 0, 0, 0))],
        out_specs=[pl.BlockSpec((1, PEER_NKEYS, tt), lambda i, h: (h, 0, i)),
                   pl.BlockSpec((1, PEER_NKEYS, tt), lambda i, h: (h, 0, i)),
                   pl.BlockSpec((1, 1, tt), lambda i, h: (h, 0, i))],
        out_shape=[jax.ShapeDtypeStruct((hh, PEER_NKEYS, n), F32),
                   jax.ShapeDtypeStruct((hh, PEER_NKEYS, n), F32),
                   jax.ShapeDtypeStruct((hh, 1, n), F32)],
        compiler_params=_params("parallel", "arbitrary"),
        name="peer_select",
    )(qry, sub_keys)


PEER_TT = 512
PEER_ET = 1024
PEER_MIX = 256


def _peer_dense_kernel(xt_ref, u_ref, vt_ref, e1_ref, e2_ref, thr_ref, o_ref,
                       acts_ref, g_ref):
    @pl.when(pl.program_id(1) == 0)
    def _():
        o_ref[...] = jnp.zeros_like(o_ref)

    xt = xt_ref[...]
    n_rows = PEER_ET // PEER_NKEYS

    def activations(r):
        sl = slice(r * PEER_NKEYS, (r + 1) * PEER_NKEYS)
        acts_ref[sl, :] = jnp.dot(u_ref[sl, :], xt, preferred_element_type=F32)

    def gate_times_gelu(r):
        w = None
        for h in range(PEER_HEADS):
            p = e1_ref[h, r:r + 1, :] * e2_ref[h]
            c = jnp.where(p >= thr_ref[h], p, 0.0)
            w = c if w is None else w + c
        sl = slice(r * PEER_NKEYS, (r + 1) * PEER_NKEYS)
        a = acts_ref[sl, :]
        gelu = 0.5 * a * (1.0 + lax.erf(a * (2.0 ** -0.5)))
        g_ref[sl, :] = (w * gelu).astype(BF16)

    def mix(q):
        sl = slice(q * PEER_MIX, (q + 1) * PEER_MIX)
        o_ref[...] += jnp.dot(vt_ref[:, sl], g_ref[sl, :], preferred_element_type=F32)

    per_mix = PEER_MIX // PEER_NKEYS
    activations(0)
    activations(1)
    for r in range(n_rows):
        if r + 2 < n_rows:
            activations(r + 2)
        gate_times_gelu(r)
        if (r + 1) % per_mix == 0:
            mix(r // per_mix)


def peer_dense(xn_t, u, v_t, e1, e2, thr):
    d, n = xn_t.shape
    ne = u.shape[0]
    rows = PEER_ET // PEER_NKEYS
    return pl.pallas_call(
        _peer_dense_kernel,
        grid=(n // PEER_TT, ne // PEER_ET),
        in_specs=[pl.BlockSpec((d, PEER_TT), lambda i, j: (0, i)),
                  pl.BlockSpec((PEER_ET, d), lambda i, j: (j, 0)),
                  pl.BlockSpec((d, PEER_ET), lambda i, j: (0, j)),
                  pl.BlockSpec((PEER_HEADS, rows, PEER_TT), lambda i, j: (0, j, i)),
                  pl.BlockSpec((PEER_HEADS, PEER_NKEYS, PEER_TT), lambda i, j: (0, 0, i)),
                  pl.BlockSpec((PEER_HEADS, 1, PEER_TT), lambda i, j: (0, 0, i))],
        out_specs=pl.BlockSpec((d, PEER_TT), lambda i, j: (0, i)),
        out_shape=jax.ShapeDtypeStruct((d, n), F32),
        scratch_shapes=[pltpu.VMEM((PEER_ET, PEER_TT), F32),
                        pltpu.VMEM((PEER_ET, PEER_TT), BF16)],
        compiler_params=_params("parallel", "arbitrary"),
        name="peer_dense",
    )(xn_t, u, v_t, e1, e2, thr)


def _add_t_kernel(h_ref, ft_ref, g_ref, o_ref, *, final_norm):
    y = h_ref[...] + ft_ref[...].T
    if final_norm:
        y = y * lax.rsqrt(jnp.mean(y * y, axis=-1, keepdims=True) + RMS_EPS) * g_ref[...]
    o_ref[...] = y


def add_transposed(h, ff_t, g, *, tm, final_norm):
    n, d = h.shape
    return pl.pallas_call(
        functools.partial(_add_t_kernel, final_norm=final_norm),
        grid=(n // tm,),
        in_specs=[pl.BlockSpec((tm, d), lambda i: (i, 0)),
                  pl.BlockSpec((d, tm), lambda i: (0, i)),
                  pl.BlockSpec((1, d), lambda i: (0, 0))],
        out_specs=pl.BlockSpec((tm, d), lambda i: (i, 0)),
        out_shape=jax.ShapeDtypeStruct((n, d), F32),
        compiler_params=_params("parallel"),
        name="add_t_norm" if final_norm else "add_t",
    )(h, ff_t, g.reshape(1, d))


def _peer_layer(h, g_ffn, w_query, sub_keys, expert_u, expert_v, g_after, final_norm):
    qry, xn = rms_matmul(h, g_ffn, w_query.astype(BF16), tm=512, tn=1024, emit_xn=True)
    e1, e2, thr = peer_select(qry, sub_keys, tt=512)
    ff_t = peer_dense(xn.T, expert_u.astype(BF16), expert_v.astype(BF16).T, e1, e2, thr)
    return add_transposed(h, ff_t, g_after, tm=512, final_norm=final_norm)


def _gla_layer(h, g_mix, w_in, w_gate_up, b_gate, g_norm, w_out, *, batch, seq):
    pad = GLA_IN_PAD - w_in.shape[1]
    w_in_pad = jnp.pad(w_in, ((0, 0), (0, pad))).astype(BF16)
    w_gate_pad = jnp.pad(w_gate_up, ((0, 128 - GLA_RANK), (0, 0)))
    proj = rms_matmul(h, g_mix, w_in_pad, tm=512, tn=640)
    o = gla_core(proj, w_gate_pad, b_gate, g_norm, batch=batch, seq=seq)
    return matmul_res(o, w_out.astype(BF16), h, tm=512, tn=1024)


def _diff_layer(h, positions, g_mix, w_in, lam_vecs, g_norm, w_out, lambda_init, *, batch, seq):
    proj = rms_matmul(h, g_mix, w_in.astype(BF16), tm=512, tn=1024)
    qt, k, vt = rope_prep(proj, positions.reshape(-1), tm=512)
    o = diff_flash(qt, k, vt, lam_vecs, g_norm, batch=batch, seq=seq, lambda_init=lambda_init)
    return matmul_res(o, w_out.astype(BF16), h, tm=512, tn=1024)


def kernel(x, positions, norm_mix, norm_ffn, norm_final, gla_w_in, gla_w_gate_up, gla_b_gate, gla_norm, gla_w_out, diff_w_in, diff_lambda_q1, diff_lambda_k1, diff_lambda_q2, diff_lambda_k2, diff_norm, diff_w_out, peer_w_query, peer_sub_keys, peer_u, peer_v):
    batch, seq, d = x.shape
    depth = norm_mix.shape[0]
    h = x.reshape(batch * seq, d)
    for i in range(depth):
        j = i // 2
        if i % 2 == 0:
            h = _gla_layer(h, norm_mix[i], gla_w_in[j], gla_w_gate_up[j], gla_b_gate[j],
                           gla_norm[j], gla_w_out[j], batch=batch, seq=seq)
        else:
            lambda_init = 0.8 - 0.6 * math.exp(-0.3 * i)
            lam_vecs = jnp.stack([diff_lambda_q1[j], diff_lambda_k1[j],
                                  diff_lambda_q2[j], diff_lambda_k2[j]])
            h = _diff_layer(h, positions, norm_mix[i], diff_w_in[j], lam_vecs, diff_norm[j],
                            diff_w_out[j], lambda_init, batch=batch, seq=seq)
        last = i == depth - 1
        h = _peer_layer(h, norm_ffn[i], peer_w_query[i], peer_sub_keys[i], peer_u[i], peer_v[i],
                        norm_final, final_norm=last)
    return h.reshape(batch, seq, d)
```

```python
import functools
import math

import jax
import jax.numpy as jnp
from jax import lax
from jax.experimental import pallas as pl
from jax.experimental.pallas import tpu as pltpu

D_MODEL = 1024
RMS_EPS = 1e-6

GLA_HEADS = 4
GLA_DK = 128
GLA_DV = 256
GLA_RANK = 16
GLA_TAU = 16.0
GLA_CHUNK = 64
GLA_QK = GLA_HEADS * GLA_DK
GLA_V = GLA_HEADS * GLA_DV
GLA_IN_PAD = 2 * GLA_QK + 2 * GLA_V + 128

DIFF_HEADS = 8
DIFF_DH = 64
DIFF_DV = 128
ROPE_DIMS = 16
ROPE_THETA = 500000.0
LOG2E = 1.4426950408889634
ATT_TQ = 1024
ATT_TK = 1024
ATT_ONES = 16
ATT_VROWS = DIFF_DV + ATT_ONES
ATT_SUB = 256

PEER_HEADS = 8
PEER_NKEYS = 128
PEER_TOPK = 16

V7X_VMEM_LIMIT_BYTES = 48 * 1024 * 1024
BF16 = jnp.bfloat16
F32 = jnp.float32
NT_DIMS = (((1,), (1,)), ((), ()))
TN_DIMS = (((0,), (0,)), ((), ()))


def _params(*sem):
    return pltpu.CompilerParams(dimension_semantics=sem,
                                vmem_limit_bytes=V7X_VMEM_LIMIT_BYTES)


def _rms_matmul_kernel(x_ref, g_ref, w_ref, o_ref, *rest, emit_xn):
    if emit_xn:
        xn_out_ref, xn_ref = rest
    else:
        (xn_ref,) = rest

    @pl.when(pl.program_id(1) == 0)
    def _():
        x = x_ref[...]
        y = x * lax.rsqrt(jnp.mean(x * x, axis=-1, keepdims=True) + RMS_EPS)
        xn_ref[...] = (y * g_ref[...]).astype(BF16)

    if emit_xn:
        xn_out_ref[...] = xn_ref[...]
    o_ref[...] = jnp.dot(xn_ref[...], w_ref[...], preferred_element_type=F32)


def rms_matmul(x, g, w, *, tm, tn, emit_xn=False):
    n, d = x.shape
    m = w.shape[1]
    out_shape = [jax.ShapeDtypeStruct((n, m), F32)]
    out_specs = [pl.BlockSpec((tm, tn), lambda i, j: (i, j))]
    if emit_xn:
        out_shape.append(jax.ShapeDtypeStruct((n, d), BF16))
        out_specs.append(pl.BlockSpec((tm, d), lambda i, j: (i, 0)))
    res = pl.pallas_call(
        functools.partial(_rms_matmul_kernel, emit_xn=emit_xn),
        grid=(n // tm, m // tn),
        in_specs=[pl.BlockSpec((tm, d), lambda i, j: (i, 0)),
                  pl.BlockSpec((1, d), lambda i, j: (0, 0)),
                  pl.BlockSpec((d, tn), lambda i, j: (0, j))],
        out_specs=out_specs,
        out_shape=out_shape,
        scratch_shapes=[pltpu.VMEM((tm, d), BF16)],
        compiler_params=_params("parallel", "arbitrary"),
        name="rms_matmul_xn" if emit_xn else "rms_matmul",
    )(x, g.reshape(1, d), w)
    return res if emit_xn else res[0]


def _matmul_res_kernel(a_ref, w_ref, r_ref, o_ref):
    o_ref[...] = r_ref[...] + jnp.dot(a_ref[...], w_ref[...], preferred_element_type=F32)


def matmul_res(a, w, res, *, tm, tn):
    n, k = a.shape
    m = w.shape[1]
    return pl.pallas_call(
        _matmul_res_kernel,
        grid=(n // tm, m // tn),
        in_specs=[pl.BlockSpec((tm, k), lambda i, j: (i, 0)),
                  pl.BlockSpec((k, tn), lambda i, j: (0, j)),
                  pl.BlockSpec((tm, tn), lambda i, j: (i, j))],
        out_specs=pl.BlockSpec((tm, tn), lambda i, j: (i, j)),
        out_shape=jax.ShapeDtypeStruct((n, m), F32),
        compiler_params=_params("parallel", "arbitrary"),
        name="matmul_res",
    )(a, w, res)


GLA_STEP = 512


def _gla_kernel(q_ref, k_ref, v_ref, r_ref, glr_ref, wg_ref, bg_ref, gn_ref,
                o_ref, state_ref):
    @pl.when(pl.program_id(1) == 0)
    def _():
        state_ref[...] = jnp.zeros_like(state_ref)

    c = GLA_CHUNK
    wg = wg_ref[...].astype(BF16)
    bg = bg_ref[...]
    row = lax.broadcasted_iota(jnp.int32, (c, c), 0)
    col = lax.broadcasted_iota(jnp.int32, (c, c), 1)
    causal = col <= row
    tril = causal.astype(BF16)

    def chunk(ci, carry):
        sl = pl.ds(pl.multiple_of(ci * c, c), c)
        q_all = q_ref[sl, :] * (GLA_DK ** -0.5)
        k_all = k_ref[sl, :]
        v_all = v_ref[sl, :].astype(BF16)
        r_all = r_ref[sl, :]
        z = jnp.dot(glr_ref[sl, :].astype(BF16), wg, preferred_element_type=F32) + bg
        glog = jax.nn.log_sigmoid(z) / GLA_TAU
        g_hi = glog.astype(BF16)
        g_lo = (glog - g_hi.astype(F32)).astype(BF16)
        cum_all = (jnp.dot(tril, g_hi, preferred_element_type=F32)
                   + jnp.dot(tril, g_lo, preferred_element_type=F32))
        outs = []
        for h in range(GLA_HEADS):
            ks = slice(h * GLA_DK, (h + 1) * GLA_DK)
            vs = slice(h * GLA_DV, (h + 1) * GLA_DV)
            q, k, v, cum = q_all[:, ks], k_all[:, ks], v_all[:, vs], cum_all[:, ks]
            last = cum[c - 1:c, :]
            mid = cum[c // 2 - 1:c // 2, :]
            qe = (q * jnp.exp(cum - mid)).astype(BF16)
            ke = (k * jnp.exp(mid - cum)).astype(BF16)
            a = lax.dot_general(qe, ke, NT_DIMS, preferred_element_type=F32)
            a = jnp.where(causal, a, 0.0).astype(BF16)
            o = jnp.dot(a, v, preferred_element_type=F32)
            st = state_ref[h]
            qc = (q * jnp.exp(cum)).astype(BF16)
            o = o + lax.dot_general(qc, st.astype(BF16), NT_DIMS, preferred_element_type=F32)
            kd = (k * jnp.exp(last - cum)).astype(BF16)
            kv_t = lax.dot_general(v, kd, TN_DIMS, preferred_element_type=F32)
            state_ref[h] = st * jnp.exp(last) + kv_t
            y = o * lax.rsqrt(jnp.mean(o * o, axis=-1, keepdims=True) + RMS_EPS) * gn_ref[h]
            r = r_all[:, vs]
            outs.append((y * (r * jax.nn.sigmoid(r))).astype(o_ref.dtype))
        o_ref[sl, :] = jnp.concatenate(outs, axis=1)
        return carry

    lax.fori_loop(0, GLA_STEP // c, chunk, 0)


def gla_core(proj, w_gate_up_pad, b_gate, g_norm, *, batch, seq):
    n = proj.shape[0]
    steps = seq // GLA_STEP

    def rows(b, s):
        return b * steps + s

    return pl.pallas_call(
        _gla_kernel,
        grid=(batch, steps),
        in_specs=[
            pl.BlockSpec((GLA_STEP, GLA_QK), lambda b, s: (rows(b, s), 0)),
            pl.BlockSpec((GLA_STEP, GLA_QK), lambda b, s: (rows(b, s), 1)),
            pl.BlockSpec((GLA_STEP, GLA_V), lambda b, s: (rows(b, s), 1)),
            pl.BlockSpec((GLA_STEP, GLA_V), lambda b, s: (rows(b, s), 2)),
            pl.BlockSpec((GLA_STEP, 128), lambda b, s: (rows(b, s), (2 * GLA_QK + 2 * GLA_V) // 128)),
            pl.BlockSpec((128, GLA_QK), lambda b, s: (0, 0)),
            pl.BlockSpec((1, GLA_QK), lambda b, s: (0, 0)),
            pl.BlockSpec((GLA_HEADS, 1, GLA_DV), lambda b, s: (0, 0, 0)),
        ],
        out_specs=pl.BlockSpec((GLA_STEP, GLA_V), lambda b, s: (rows(b, s), 0)),
        out_shape=jax.ShapeDtypeStruct((n, GLA_V), BF16),
        scratch_shapes=[pltpu.VMEM((GLA_HEADS, GLA_DV, GLA_DK), F32)],
        compiler_params=_params("parallel", "arbitrary"),
        name="gla_core",
    )(proj, proj, proj, proj, proj, w_gate_up_pad, b_gate.reshape(1, GLA_QK),
      g_norm.reshape(GLA_HEADS, 1, GLA_DV))


def _rope_kernel(pos_ref, freq_ref, q_ref, k_ref, v_ref, qo_ref, ko_ref, vo_ref):
    ang = pos_ref[...].astype(F32) * freq_ref[0:1, :]
    cos = jnp.cos(ang)
    sin = jnp.sin(ang)
    sgn_lo = freq_ref[1:2, :]
    sgn_hi = freq_ref[2:3, :]
    half = ROPE_DIMS // 2

    def rot(t, scale):
        outs = []
        for h in range(DIFF_HEADS):
            x = t[:, h * 128:(h + 1) * 128]
            partner = (sgn_lo * pltpu.roll(x, 128 - half, 1)
                       + sgn_hi * pltpu.roll(x, half, 1))
            outs.append((x * cos + partner * sin) * scale)
        return jnp.concatenate(outs, axis=1)

    qo_ref[...] = rot(q_ref[...], DIFF_DH ** -0.5 * LOG2E).T.astype(BF16)
    ko_ref[...] = rot(k_ref[...], 1.0).astype(BF16)
    vt = v_ref[...].T
    ones = jnp.ones((ATT_ONES, vt.shape[1]), F32)
    vo_ref[...] = jnp.concatenate(
        [piece for h in range(DIFF_HEADS)
         for piece in (vt[h * DIFF_DV:(h + 1) * DIFF_DV], ones)], axis=0).astype(BF16)


def rope_prep(proj, positions, *, tm):
    n = proj.shape[0]
    w = DIFF_HEADS * 128
    lane = jnp.arange(128) % DIFF_DH
    inv_freq = ROPE_THETA ** (-jnp.arange(0, ROPE_DIMS, 2, dtype=F32) / ROPE_DIMS)
    half = ROPE_DIMS // 2
    freq = jnp.where(lane < ROPE_DIMS, inv_freq[lane % half], 0.0)
    sgn_lo = jnp.where(lane < half, -1.0, 0.0)
    sgn_hi = jnp.where((lane >= half) & (lane < ROPE_DIMS), 1.0, 0.0)
    table = jnp.zeros((8, 128), F32).at[0].set(freq).at[1].set(sgn_lo).at[2].set(sgn_hi)
    spec_in = lambda c: pl.BlockSpec((tm, w), lambda i: (i, c))
    row_major = jax.ShapeDtypeStruct((n, w), BF16)
    transposed = jax.ShapeDtypeStruct((w, n), BF16)
    spec_t = pl.BlockSpec((w, tm), lambda i: (0, i))
    wv = DIFF_HEADS * ATT_VROWS
    return pl.pallas_call(
        _rope_kernel,
        grid=(n // tm,),
        in_specs=[pl.BlockSpec((tm, 1), lambda i: (i, 0)),
                  pl.BlockSpec((8, 128), lambda i: (0, 0)),
                  spec_in(0), spec_in(1), spec_in(2)],
        out_specs=[spec_t, pl.BlockSpec((tm, w), lambda i: (i, 0)),
                   pl.BlockSpec((wv, tm), lambda i: (0, i))],
        out_shape=[transposed, row_major, jax.ShapeDtypeStruct((wv, n), BF16)],
        compiler_params=_params("parallel"),
        name="rope_prep",
    )(positions.reshape(n, 1), table, proj, proj, proj)


def _flash_kernel(qi_ref, kj_ref, lam_ref, qt_ref, k_ref, vt_ref, gn_ref, o_ref,
                  m_ref, acc_ref, s_ref, *, lambda_init):
    step = pl.program_id(2)
    qi = qi_ref[step]
    kj = kj_ref[step]
    last_kj = (qi * ATT_TQ) // ATT_TK

    @pl.when(kj == 0)
    def _():
        m_ref[...] = jnp.full_like(m_ref, -jnp.inf)
        acc_ref[...] = jnp.zeros_like(acc_ref)

    def accumulate(masked):
        qt = qt_ref[...]
        lane = lax.broadcasted_iota(jnp.int32, (ATT_SUB, 128), 1)
        if masked:
            key = lax.broadcasted_iota(jnp.int32, (ATT_SUB, ATT_TQ), 0)
            qry = lax.broadcasted_iota(jnp.int32, (ATT_SUB, ATT_TQ), 1)
            visible0 = key <= qry + (qi * ATT_TQ - kj * ATT_TK)

        def scores(c):
            k = k_ref[c * ATT_SUB:(c + 1) * ATT_SUB, :]
            for m in range(2):
                in_map = (lane < DIFF_DH) if m == 0 else (lane >= DIFF_DH)
                km = jnp.where(in_map, k, jnp.zeros_like(k))
                s_ref[m, c] = jnp.dot(km, qt, preferred_element_type=F32)

        def softmax_pv(c):
            vt = vt_ref[:, c * ATT_SUB:(c + 1) * ATT_SUB]
            for m in range(2):
                s = s_ref[m, c]
                if masked:
                    visible = visible0 if c == 0 else (key + c * ATT_SUB
                                                       <= qry + (qi * ATT_TQ - kj * ATT_TK))
                    s = jnp.where(visible, s, -jnp.inf)
                m_old = m_ref[m]
                m_new = jnp.maximum(m_old, jnp.max(s, axis=0, keepdims=True))
                p = jnp.exp2((s - m_new).astype(BF16))
                alpha = jnp.exp2(m_old - m_new)
                acc_ref[m] = alpha * acc_ref[m] + jnp.dot(vt, p, preferred_element_type=F32)
                m_ref[m] = m_new

        n_sub = ATT_TK // ATT_SUB
        scores(0)
        for c in range(n_sub):
            if c + 1 < n_sub:
                scores(c + 1)
            softmax_pv(c)

    @pl.when(kj < last_kj)
    def _():
        accumulate(False)

    @pl.when(kj == last_kj)
    def _():
        accumulate(True)
        lv = lam_ref[...]
        lam = (jnp.exp(jnp.sum(lv[0:1] * lv[1:2], axis=1, keepdims=True))
               - jnp.exp(jnp.sum(lv[2:3] * lv[3:4], axis=1, keepdims=True)) + lambda_init)
        a0 = acc_ref[0]
        a1 = acc_ref[1]
        o = (a0[:DIFF_DV] / a0[DIFF_DV:DIFF_DV + 1]
             - lam * (a1[:DIFF_DV] / a1[DIFF_DV:DIFF_DV + 1]))
        y = o * lax.rsqrt(jnp.mean(o * o, axis=0, keepdims=True) + RMS_EPS)
        y = y * (gn_ref[0] * (1.0 - lambda_init))
        o_ref[...] = y.T.astype(o_ref.dtype)


def diff_flash(qt, k, vt, lam_vecs, g_norm, *, batch, seq, lambda_init):
    n = k.shape[0]
    nq = seq // ATT_TQ
    nk = seq // ATT_TK
    pairs = [(i, j) for i in range(nq) for j in range((i * ATT_TQ) // ATT_TK + 1)]
    qi = jnp.asarray([p[0] for p in pairs], jnp.int32)
    kj = jnp.asarray([p[1] for p in pairs], jnp.int32)
    lam_tab = jnp.zeros((8, 128), F32).at[0:4, 0:DIFF_DH].set(lam_vecs)
    grid_spec = pltpu.PrefetchScalarGridSpec(
        num_scalar_prefetch=2,
        grid=(batch, DIFF_HEADS, len(pairs)),
        in_specs=[
            pl.BlockSpec((8, 128), lambda b, h, s, qi, kj: (0, 0)),
            pl.BlockSpec((128, ATT_TQ), lambda b, h, s, qi, kj: (h, b * nq + qi[s])),
            pl.BlockSpec((ATT_TK, 128), lambda b, h, s, qi, kj: (b * nk + kj[s], h)),
            pl.BlockSpec((ATT_VROWS, ATT_TK), lambda b, h, s, qi, kj: (h, b * nk + kj[s])),
            pl.BlockSpec((1, DIFF_DV, 1), lambda b, h, s, qi, kj: (h, 0, 0)),
        ],
        out_specs=pl.BlockSpec((ATT_TQ, 128), lambda b, h, s, qi, kj: (b * nq + qi[s], h)),
        scratch_shapes=[pltpu.VMEM((2, 1, ATT_TQ), F32),
                        pltpu.VMEM((2, ATT_VROWS, ATT_TQ), F32),
                        pltpu.VMEM((2, ATT_TK // ATT_SUB, ATT_SUB, ATT_TQ), F32)],
    )
    return pl.pallas_call(
        functools.partial(_flash_kernel, lambda_init=lambda_init),
        grid_spec=grid_spec,
        out_shape=jax.ShapeDtypeStruct((n, DIFF_HEADS * DIFF_DV), BF16),
        compiler_params=_params("parallel", "parallel", "arbitrary"),
        name="diff_flash",
    )(qi, kj, lam_tab, qt, k, vt, g_norm.reshape(DIFF_HEADS, DIFF_DV, 1))


def _compare_exchange(x, i, j):
    hi, lo = jnp.maximum(x[i], x[j]), jnp.minimum(x[i], x[j])
    x[i], x[j] = hi, lo


def _bitonic_merge_desc(x):
    n = len(x)
    j = n // 2
    while j >= 1:
        for i in range(n):
            if i & j == 0:
                _compare_exchange(x, i, i | j)
        j //= 2


def _bitonic_sort_desc(x):
    n = len(x)
    k = 2
    while k <= n:
        j = k // 2
        while j >= 1:
            for i in range(n):
                l = i ^ j
                if l > i:
                    if i & k == 0:
                        _compare_exchange(x, i, l)
                    else:
                        _compare_exchange(x, l, i)
            j //= 2
        k *= 2


def _top16_per_column(slabs):
    x = list(slabs)
    n = len(x)
    _bitonic_sort_desc(x)
    for shift in (4, 2, 1):
        partner = [pltpu.roll(v, shift, 0) for v in x]
        x = [jnp.maximum(x[i], partner[n - 1 - i]) for i in range(n)]
        _bitonic_merge_desc(x)
    return [v[0:1, :] for v in x]


def _peer_select_kernel(q_ref, keys_ref, e1_ref, d1_ref, e2_ref, c2_ref):
    k_top = PEER_TOPK
    q = q_ref[...].astype(BF16)
    keys = keys_ref[0].astype(BF16)
    s1 = lax.dot_general(keys[0], q[:, :128], NT_DIMS, preferred_element_type=F32)
    s2 = lax.dot_general(keys[1], q[:, 128:], NT_DIMS, preferred_element_type=F32)
    a = _top16_per_column([s1[8 * g:8 * g + 8, :] for g in range(PEER_NKEYS // 8)])
    b = _top16_per_column([s2[8 * g:8 * g + 8, :] for g in range(PEER_NKEYS // 8)])
    pairs = [(i, j) for i in range(k_top) for j in range(k_top) if (i + 1) * (j + 1) <= k_top]
    cands = [a[i] + b[j] for i, j in pairs]
    neg = jnp.full_like(a[0], -jnp.inf)
    cands += [neg] * (4 * k_top - len(cands))
    slabs = [jnp.concatenate(cands[4 * g:4 * g + 4] + [neg] * 4, axis=0) for g in range(k_top)]
    best = _top16_per_column(slabs)
    top = best[0]
    z = jnp.exp(best[0] - top)
    for r in range(1, k_top):
        z = z + jnp.exp(best[r] - top)
    inv_z = 1.0 / z
    kth = best[k_top - 1]
    d1 = jnp.full_like(s1, float(k_top + 1))
    for i in range(k_top):
        n_i = sum(jnp.where(a[i] + b[j] >= kth, 1.0, 0.0) for (ii, j) in pairs if ii == i)
        d1 = jnp.where(s1 == a[i], float(k_top + 1) - n_i, d1)
    c2 = jnp.zeros_like(s2)
    for r in range(k_top):
        c2 = c2 + jnp.where(s2 >= b[r], 1.0, 0.0)
    e1_ref[0] = jnp.where(s1 >= a[k_top - 1], jnp.exp(s1 - a[0]), 0.0)
    d1_ref[0] = d1
    e2_ref[0] = jnp.where(c2 > 0.0, jnp.exp(s2 - b[0]) * inv_z, 0.0).astype(BF16)
    c2_ref[0] = c2.astype(BF16)


def peer_select(qry, sub_keys, *, tt):
    n = qry.shape[0]
    hh = PEER_HEADS
    spec = pl.BlockSpec((1, PEER_NKEYS, tt), lambda i, h: (h, 0, i))
    shape = lambda dt: jax.ShapeDtypeStruct((hh, PEER_NKEYS, n), dt)
    return pl.pallas_call(
        _peer_select_kernel,
        grid=(n // tt, hh),
        in_specs=[pl.BlockSpec((tt, 256), lambda i, h: (i, h)),
                  pl.BlockSpec((1, 2, PEER_NKEYS, 128), lambda i, h: (h, 0, 0, 0))],
        out_specs=[spec, spec, spec, spec],
        out_shape=[shape(F32), shape(F32), shape(BF16), shape(BF16)],
        compiler_params=_params("parallel", "arbitrary"),
        name="peer_select",
    )(qry, sub_keys)


PEER_TT = 512
PEER_ET = 1024
PEER_MIX = 256
BF16_ROWS = 16


def _peer_dense_kernel(xt_ref, u_ref, vt_ref, e1_ref, d1_ref, e2_ref, c2_ref, o_ref,
                       acts_ref, g_ref):
    @pl.when(pl.program_id(1) == 0)
    def _():
        o_ref[...] = jnp.zeros_like(o_ref)

    xt = xt_ref[...]
    n_rows = PEER_ET // PEER_NKEYS

    def activations(r):
        sl = slice(r * PEER_NKEYS, (r + 1) * PEER_NKEYS)
        acts_ref[sl, :] = jnp.dot(u_ref[sl, :], xt, preferred_element_type=F32)

    def gates(r):
        tiles = (PEER_NKEYS // BF16_ROWS, BF16_ROWS, PEER_TT)
        w = None
        for h in range(PEER_HEADS):
            e1 = jnp.broadcast_to(e1_ref[h, r:r + 1, :], tiles[1:]).astype(BF16)
            d1 = jnp.broadcast_to(d1_ref[h, r:r + 1, :], tiles[1:]).astype(BF16)
            e2 = e2_ref[h].reshape(tiles)
            c2 = c2_ref[h].reshape(tiles)
            p = e1[None] * e2
            c = jnp.where(c2 >= jnp.broadcast_to(d1[None], tiles), p, jnp.zeros_like(p))
            w = c if w is None else w + c
        return w.reshape(PEER_NKEYS, PEER_TT)

    def times_gelu(r, w):
        sl = slice(r * PEER_NKEYS, (r + 1) * PEER_NKEYS)
        a = acts_ref[sl, :]
        gelu = 0.5 * a * (1.0 + lax.erf(a * (2.0 ** -0.5)))
        g_ref[sl, :] = w * gelu.astype(BF16)

    def mix(q):
        sl = slice(q * PEER_MIX, (q + 1) * PEER_MIX)
        o_ref[...] += jnp.dot(vt_ref[:, sl], g_ref[sl, :], preferred_element_type=F32)

    per_mix = PEER_MIX // PEER_NKEYS
    activations(0)
    w = gates(0)
    activations(1)
    for r in range(n_rows):
        if r + 2 < n_rows:
            activations(r + 2)
        times_gelu(r, w)
        if r + 1 < n_rows:
            w = gates(r + 1)
        if (r + 1) % per_mix == 0:
            mix(r // per_mix)


def peer_dense(xn_t, u, v_t, e1, d1, e2, c2):
    d, n = xn_t.shape
    ne = u.shape[0]
    rows = PEER_ET // PEER_NKEYS
    per_row = pl.BlockSpec((PEER_HEADS, rows, PEER_TT), lambda i, j: (0, j, i))
    per_col = pl.BlockSpec((PEER_HEADS, PEER_NKEYS, PEER_TT), lambda i, j: (0, 0, i))
    return pl.pallas_call(
        _peer_dense_kernel,
        grid=(n // PEER_TT, ne // PEER_ET),
        in_specs=[pl.BlockSpec((d, PEER_TT), lambda i, j: (0, i)),
                  pl.BlockSpec((PEER_ET, d), lambda i, j: (j, 0)),
                  pl.BlockSpec((d, PEER_ET), lambda i, j: (0, j)),
                  per_row, per_row, per_col, per_col],
        out_specs=pl.BlockSpec((d, PEER_TT), lambda i, j: (0, i)),
        out_shape=jax.ShapeDtypeStruct((d, n), F32),
        scratch_shapes=[pltpu.VMEM((PEER_ET, PEER_TT), F32),
                        pltpu.VMEM((PEER_ET, PEER_TT), BF16)],
        compiler_params=_params("parallel", "arbitrary"),
        name="peer_dense",
    )(xn_t, u, v_t, e1, d1, e2, c2)


def _add_t_kernel(h_ref, ft_ref, g_ref, o_ref, *, final_norm):
    y = h_ref[...] + ft_ref[...].T
    if final_norm:
        y = y * lax.rsqrt(jnp.mean(y * y, axis=-1, keepdims=True) + RMS_EPS) * g_ref[...]
    o_ref[...] = y


def add_transposed(h, ff_t, g, *, tm, final_norm):
    n, d = h.shape
    return pl.pallas_call(
        functools.partial(_add_t_kernel, final_norm=final_norm),
        grid=(n // tm,),
        in_specs=[pl.BlockSpec((tm, d), lambda i: (i, 0)),
                  pl.BlockSpec((d, tm), lambda i: (0, i)),
                  pl.BlockSpec((1, d), lambda i: (0, 0))],
        out_specs=pl.BlockSpec((tm, d), lambda i: (i, 0)),
        out_shape=jax.ShapeDtypeStruct((n, d), F32),
        compiler_params=_params("parallel"),
        name="add_t_norm" if final_norm else "add_t",
    )(h, ff_t, g.reshape(1, d))


def _peer_layer(h, g_ffn, w_query, sub_keys, expert_u, expert_v, g_after, final_norm):
    qry, xn = rms_matmul(h, g_ffn, w_query.astype(BF16), tm=512, tn=1024, emit_xn=True)
    e1, d1, e2, c2 = peer_select(qry, sub_keys, tt=512)
    ff_t = peer_dense(xn.T, expert_u.astype(BF16), expert_v.astype(BF16).T, e1, d1, e2, c2)
    return add_transposed(h, ff_t, g_after, tm=512, final_norm=final_norm)


def _gla_layer(h, g_mix, w_in, w_gate_up, b_gate, g_norm, w_out, *, batch, seq):
    pad = GLA_IN_PAD - w_in.shape[1]
    w_in_pad = jnp.pad(w_in, ((0, 0), (0, pad))).astype(BF16)
    w_gate_pad = jnp.pad(w_gate_up, ((0, 128 - GLA_RANK), (0, 0)))
    proj = rms_matmul(h, g_mix, w_in_pad, tm=512, tn=640)
    o = gla_core(proj, w_gate_pad, b_gate, g_norm, batch=batch, seq=seq)
    return matmul_res(o, w_out.astype(BF16), h, tm=512, tn=1024)


def _diff_layer(h, positions, g_mix, w_in, lam_vecs, g_norm, w_out, lambda_init, *, batch, seq):
    proj = rms_matmul(h, g_mix, w_in.astype(BF16), tm=512, tn=1024)
    qt, k, vt = rope_prep(proj, positions.reshape(-1), tm=512)
    o = diff_flash(qt, k, vt, lam_vecs, g_norm, batch=batch, seq=seq, lambda_init=lambda_init)
    return matmul_res(o, w_out.astype(BF16), h, tm=512, tn=1024)


def kernel(x, positions, norm_mix, norm_ffn, norm_final, gla_w_in, gla_w_gate_up, gla_b_gate, gla_norm, gla_w_out, diff_w_in, diff_lambda_q1, diff_lambda_k1, diff_lambda_q2, diff_lambda_k2, diff_norm, diff_w_out, peer_w_query, peer_sub_keys, peer_u, peer_v):
    batch, seq, d = x.shape
    depth = norm_mix.shape[0]
    h = x.reshape(batch * seq, d)
    for i in range(depth):
        j = i // 2
        if i % 2 == 0:
            h = _gla_layer(h, norm_mix[i], gla_w_in[j], gla_w_gate_up[j], gla_b_gate[j],
                           gla_norm[j], gla_w_out[j], batch=batch, seq=seq)
        else:
            lambda_init = 0.8 - 0.6 * math.exp(-0.3 * i)
            lam_vecs = jnp.stack([diff_lambda_q1[j], diff_lambda_k1[j],
                                  diff_lambda_q2[j], diff_lambda_k2[j]])
            h = _diff_layer(h, positions, norm_mix[i], diff_w_in[j], lam_vecs, diff_norm[j],
                            diff_w_out[j], lambda_init, batch=batch, seq=seq)
        last = i == depth - 1
        h = _peer_layer(h, norm_ffn[i], peer_w_query[i], peer_sub_keys[i], peer_u[i], peer_v[i],
                        norm_final, final_norm=last)
    return h.reshape(batch, seq, d)
```

```python
import functools
import math

import jax
import jax.numpy as jnp
from jax import lax
from jax.experimental import pallas as pl
from jax.experimental.pallas import tpu as pltpu

D_MODEL = 1024
RMS_EPS = 1e-6

GLA_HEADS = 4
GLA_DK = 128
GLA_DV = 256
GLA_RANK = 16
GLA_TAU = 16.0
GLA_CHUNK = 64
GLA_QK = GLA_HEADS * GLA_DK
GLA_V = GLA_HEADS * GLA_DV
GLA_IN_PAD = 2 * GLA_QK + 2 * GLA_V + 128

DIFF_HEADS = 8
DIFF_DH = 64
DIFF_DV = 128
ROPE_DIMS = 16
ROPE_THETA = 500000.0
LOG2E = 1.4426950408889634
ATT_TQ = 1024
ATT_TK = 1024
ATT_ONES = 16
ATT_VROWS = DIFF_DV + ATT_ONES
ATT_SUB = 256

PEER_HEADS = 8
PEER_NKEYS = 128
PEER_TOPK = 16

V7X_VMEM_LIMIT_BYTES = 48 * 1024 * 1024
BF16 = jnp.bfloat16
F32 = jnp.float32
NT_DIMS = (((1,), (1,)), ((), ()))
TN_DIMS = (((0,), (0,)), ((), ()))


def _params(*sem):
    return pltpu.CompilerParams(dimension_semantics=sem,
                                vmem_limit_bytes=V7X_VMEM_LIMIT_BYTES)


def _rms_matmul_kernel(x_ref, g_ref, w_ref, o_ref, *rest, emit_xn):
    if emit_xn:
        xn_out_ref, xn_ref = rest
    else:
        (xn_ref,) = rest

    @pl.when(pl.program_id(1) == 0)
    def _():
        x = x_ref[...]
        y = x * lax.rsqrt(jnp.mean(x * x, axis=-1, keepdims=True) + RMS_EPS)
        xn_ref[...] = (y * g_ref[...]).astype(BF16)

    if emit_xn:
        xn_out_ref[...] = xn_ref[...]
    o_ref[...] = jnp.dot(xn_ref[...], w_ref[...], preferred_element_type=F32)


def rms_matmul(x, g, w, *, tm, tn, emit_xn=False):
    n, d = x.shape
    m = w.shape[1]
    out_shape = [jax.ShapeDtypeStruct((n, m), F32)]
    out_specs = [pl.BlockSpec((tm, tn), lambda i, j: (i, j))]
    if emit_xn:
        out_shape.append(jax.ShapeDtypeStruct((n, d), BF16))
        out_specs.append(pl.BlockSpec((tm, d), lambda i, j: (i, 0)))
    res = pl.pallas_call(
        functools.partial(_rms_matmul_kernel, emit_xn=emit_xn),
        grid=(n // tm, m // tn),
        in_specs=[pl.BlockSpec((tm, d), lambda i, j: (i, 0)),
                  pl.BlockSpec((1, d), lambda i, j: (0, 0)),
                  pl.BlockSpec((d, tn), lambda i, j: (0, j))],
        out_specs=out_specs,
        out_shape=out_shape,
        scratch_shapes=[pltpu.VMEM((tm, d), BF16)],
        compiler_params=_params("parallel", "arbitrary"),
        name="rms_matmul_xn" if emit_xn else "rms_matmul",
    )(x, g.reshape(1, d), w)
    return res if emit_xn else res[0]


def _matmul_res_kernel(a_ref, w_ref, r_ref, o_ref):
    o_ref[...] = r_ref[...] + jnp.dot(a_ref[...], w_ref[...], preferred_element_type=F32)


def matmul_res(a, w, res, *, tm, tn):
    n, k = a.shape
    m = w.shape[1]
    return pl.pallas_call(
        _matmul_res_kernel,
        grid=(n // tm, m // tn),
        in_specs=[pl.BlockSpec((tm, k), lambda i, j: (i, 0)),
                  pl.BlockSpec((k, tn), lambda i, j: (0, j)),
                  pl.BlockSpec((tm, tn), lambda i, j: (i, j))],
        out_specs=pl.BlockSpec((tm, tn), lambda i, j: (i, j)),
        out_shape=jax.ShapeDtypeStruct((n, m), F32),
        compiler_params=_params("parallel", "arbitrary"),
        name="matmul_res",
    )(a, w, res)


GLA_STEP = 512


def _gla_kernel(q_ref, k_ref, v_ref, r_ref, glr_ref, wg_ref, bg_ref, gn_ref,
                o_ref, state_ref):
    @pl.when(pl.program_id(1) == 0)
    def _():
        state_ref[...] = jnp.zeros_like(state_ref)

    c = GLA_CHUNK
    wg = wg_ref[...].astype(BF16)
    bg = bg_ref[...]
    row = lax.broadcasted_iota(jnp.int32, (c, c), 0)
    col = lax.broadcasted_iota(jnp.int32, (c, c), 1)
    causal = col <= row
    tril = causal.astype(BF16)

    def chunk(ci, carry):
        sl = pl.ds(pl.multiple_of(ci * c, c), c)
        q_all = q_ref[sl, :] * (GLA_DK ** -0.5)
        k_all = k_ref[sl, :]
        v_all = v_ref[sl, :].astype(BF16)
        r_all = r_ref[sl, :]
        z = jnp.dot(glr_ref[sl, :].astype(BF16), wg, preferred_element_type=F32) + bg
        glog = jax.nn.log_sigmoid(z) / GLA_TAU
        g_hi = glog.astype(BF16)
        g_lo = (glog - g_hi.astype(F32)).astype(BF16)
        cum_all = (jnp.dot(tril, g_hi, preferred_element_type=F32)
                   + jnp.dot(tril, g_lo, preferred_element_type=F32))
        outs = []
        for h in range(GLA_HEADS):
            ks = slice(h * GLA_DK, (h + 1) * GLA_DK)
            vs = slice(h * GLA_DV, (h + 1) * GLA_DV)
            q, k, v, cum = q_all[:, ks], k_all[:, ks], v_all[:, vs], cum_all[:, ks]
            last = cum[c - 1:c, :]
            mid = cum[c // 2 - 1:c // 2, :]
            qe = (q * jnp.exp(cum - mid)).astype(BF16)
            ke = (k * jnp.exp(mid - cum)).astype(BF16)
            a = lax.dot_general(qe, ke, NT_DIMS, preferred_element_type=F32)
            a = jnp.where(causal, a, 0.0).astype(BF16)
            o = jnp.dot(a, v, preferred_element_type=F32)
            st = state_ref[h]
            qc = (q * jnp.exp(cum)).astype(BF16)
            o = o + lax.dot_general(qc, st.astype(BF16), NT_DIMS, preferred_element_type=F32)
            kd = (k * jnp.exp(last - cum)).astype(BF16)
            kv_t = lax.dot_general(v, kd, TN_DIMS, preferred_element_type=F32)
            state_ref[h] = st * jnp.exp(last) + kv_t
            y = o * lax.rsqrt(jnp.mean(o * o, axis=-1, keepdims=True) + RMS_EPS) * gn_ref[h]
            r = r_all[:, vs]
            outs.append((y * (r * jax.nn.sigmoid(r))).astype(o_ref.dtype))
        o_ref[sl, :] = jnp.concatenate(outs, axis=1)
        return carry

    lax.fori_loop(0, GLA_STEP // c, chunk, 0)


def gla_core(proj, w_gate_up_pad, b_gate, g_norm, *, batch, seq):
    n = proj.shape[0]
    steps = seq // GLA_STEP

    def rows(b, s):
        return b * steps + s

    return pl.pallas_call(
        _gla_kernel,
        grid=(batch, steps),
        in_specs=[
            pl.BlockSpec((GLA_STEP, GLA_QK), lambda b, s: (rows(b, s), 0)),
            pl.BlockSpec((GLA_STEP, GLA_QK), lambda b, s: (rows(b, s), 1)),
            pl.BlockSpec((GLA_STEP, GLA_V), lambda b, s: (rows(b, s), 1)),
            pl.BlockSpec((GLA_STEP, GLA_V), lambda b, s: (rows(b, s), 2)),
            pl.BlockSpec((GLA_STEP, 128), lambda b, s: (rows(b, s), (2 * GLA_QK + 2 * GLA_V) // 128)),
            pl.BlockSpec((128, GLA_QK), lambda b, s: (0, 0)),
            pl.BlockSpec((1, GLA_QK), lambda b, s: (0, 0)),
            pl.BlockSpec((GLA_HEADS, 1, GLA_DV), lambda b, s: (0, 0, 0)),
        ],
        out_specs=pl.BlockSpec((GLA_STEP, GLA_V), lambda b, s: (rows(b, s), 0)),
        out_shape=jax.ShapeDtypeStruct((n, GLA_V), BF16),
        scratch_shapes=[pltpu.VMEM((GLA_HEADS, GLA_DV, GLA_DK), F32)],
        compiler_params=_params("parallel", "arbitrary"),
        name="gla_core",
    )(proj, proj, proj, proj, proj, w_gate_up_pad, b_gate.reshape(1, GLA_QK),
      g_norm.reshape(GLA_HEADS, 1, GLA_DV))


def _rope_kernel(pos_ref, freq_ref, q_ref, k_ref, v_ref, qo_ref, ko_ref, vo_ref):
    ang = pos_ref[...].astype(F32) * freq_ref[0:1, :]
    cos = jnp.cos(ang)
    sin = jnp.sin(ang)
    sgn_lo = freq_ref[1:2, :]
    sgn_hi = freq_ref[2:3, :]
    half = ROPE_DIMS // 2

    def rot(t, scale):
        outs = []
        for h in range(DIFF_HEADS):
            x = t[:, h * 128:(h + 1) * 128]
            partner = (sgn_lo * pltpu.roll(x, 128 - half, 1)
                       + sgn_hi * pltpu.roll(x, half, 1))
            outs.append((x * cos + partner * sin) * scale)
        return jnp.concatenate(outs, axis=1)

    qo_ref[...] = rot(q_ref[...], DIFF_DH ** -0.5 * LOG2E).T.astype(BF16)
    ko_ref[...] = rot(k_ref[...], 1.0).astype(BF16)
    vt = v_ref[...].T
    ones = jnp.ones((ATT_ONES, vt.shape[1]), F32)
    vo_ref[...] = jnp.concatenate(
        [piece for h in range(DIFF_HEADS)
         for piece in (vt[h * DIFF_DV:(h + 1) * DIFF_DV], ones)], axis=0).astype(BF16)


def rope_prep(proj, positions, *, tm):
    n = proj.shape[0]
    w = DIFF_HEADS * 128
    lane = jnp.arange(128) % DIFF_DH
    inv_freq = ROPE_THETA ** (-jnp.arange(0, ROPE_DIMS, 2, dtype=F32) / ROPE_DIMS)
    half = ROPE_DIMS // 2
    freq = jnp.where(lane < ROPE_DIMS, inv_freq[lane % half], 0.0)
    sgn_lo = jnp.where(lane < half, -1.0, 0.0)
    sgn_hi = jnp.where((lane >= half) & (lane < ROPE_DIMS), 1.0, 0.0)
    table = jnp.zeros((8, 128), F32).at[0].set(freq).at[1].set(sgn_lo).at[2].set(sgn_hi)
    spec_in = lambda c: pl.BlockSpec((tm, w), lambda i: (i, c))
    row_major = jax.ShapeDtypeStruct((n, w), BF16)
    transposed = jax.ShapeDtypeStruct((w, n), BF16)
    spec_t = pl.BlockSpec((w, tm), lambda i: (0, i))
    wv = DIFF_HEADS * ATT_VROWS
    return pl.pallas_call(
        _rope_kernel,
        grid=(n // tm,),
        in_specs=[pl.BlockSpec((tm, 1), lambda i: (i, 0)),
                  pl.BlockSpec((8, 128), lambda i: (0, 0)),
                  spec_in(0), spec_in(1), spec_in(2)],
        out_specs=[spec_t, pl.BlockSpec((tm, w), lambda i: (i, 0)),
                   pl.BlockSpec((wv, tm), lambda i: (0, i))],
        out_shape=[transposed, row_major, jax.ShapeDtypeStruct((wv, n), BF16)],
        compiler_params=_params("parallel"),
        name="rope_prep",
    )(positions.reshape(n, 1), table, proj, proj, proj)


def _flash_kernel(lam_ref, qt_ref, k_ref, vt_ref, gn_ref, o_ref,
                  m_ref, acc_ref, s_ref, *, lambda_init):
    qi = pl.program_id(2)
    assert ATT_TQ == ATT_TK
    m_ref[...] = jnp.full_like(m_ref, -jnp.inf)
    acc_ref[...] = jnp.zeros_like(acc_ref)
    qt = qt_ref[...]
    lane = lax.broadcasted_iota(jnp.int32, (ATT_SUB, 128), 1)
    n_sub = ATT_TK // ATT_SUB

    def accumulate(kj, masked):
        base = pl.multiple_of(kj * ATT_TK, ATT_TK)

        def first_query(c):
            return c * ATT_SUB if masked else 0

        def scores(c):
            k = k_ref[pl.ds(base + c * ATT_SUB, ATT_SUB), :]
            q0 = first_query(c)
            for m in range(2):
                in_map = (lane < DIFF_DH) if m == 0 else (lane >= DIFF_DH)
                km = jnp.where(in_map, k, jnp.zeros_like(k))
                s_ref[m, c, :, q0:] = jnp.dot(km, qt[:, q0:],
                                              preferred_element_type=F32)

        def softmax_pv(c):
            vt = vt_ref[:, pl.ds(base + c * ATT_SUB, ATT_SUB)]
            q0 = first_query(c)
            for m in range(2):
                s = s_ref[m, c, :, q0:]
                if masked:
                    key = lax.broadcasted_iota(jnp.int32, s.shape, 0)
                    qry = lax.broadcasted_iota(jnp.int32, s.shape, 1)
                    s = jnp.where(key <= qry, s, -jnp.inf)
                m_old = m_ref[m, :, q0:]
                m_new = jnp.maximum(m_old, jnp.max(s, axis=0, keepdims=True))
                p = jnp.exp2((s - m_new).astype(BF16))
                alpha = jnp.exp2(m_old - m_new)
                acc_ref[m, :, q0:] = (alpha * acc_ref[m, :, q0:]
                                      + jnp.dot(vt, p, preferred_element_type=F32))
                m_ref[m, :, q0:] = m_new

        scores(0)
        for c in range(n_sub):
            if c + 1 < n_sub:
                scores(c + 1)
            softmax_pv(c)

    def full_block(kj, carry):
        accumulate(kj, False)
        return carry

    lax.fori_loop(0, qi, full_block, 0)
    accumulate(qi, True)

    lv = lam_ref[...]
    lam = (jnp.exp(jnp.sum(lv[0:1] * lv[1:2], axis=1, keepdims=True))
           - jnp.exp(jnp.sum(lv[2:3] * lv[3:4], axis=1, keepdims=True)) + lambda_init)
    a0 = acc_ref[0]
    a1 = acc_ref[1]
    o = (a0[:DIFF_DV] / a0[DIFF_DV:DIFF_DV + 1]
         - lam * (a1[:DIFF_DV] / a1[DIFF_DV:DIFF_DV + 1]))
    y = o * lax.rsqrt(jnp.mean(o * o, axis=0, keepdims=True) + RMS_EPS)
    y = y * (gn_ref[0] * (1.0 - lambda_init))
    o_ref[...] = y.T.astype(o_ref.dtype)


def diff_flash(qt, k, vt, lam_vecs, g_norm, *, batch, seq, lambda_init):
    n = k.shape[0]
    nq = seq // ATT_TQ
    lam_tab = jnp.zeros((8, 128), F32).at[0:4, 0:DIFF_DH].set(lam_vecs)
    return pl.pallas_call(
        functools.partial(_flash_kernel, lambda_init=lambda_init),
        grid=(batch, DIFF_HEADS, nq),
        in_specs=[
            pl.BlockSpec((8, 128), lambda b, h, i: (0, 0)),
            pl.BlockSpec((128, ATT_TQ), lambda b, h, i: (h, b * nq + i)),
            pl.BlockSpec((seq, 128), lambda b, h, i: (b, h)),
            pl.BlockSpec((ATT_VROWS, seq), lambda b, h, i: (h, b)),
            pl.BlockSpec((1, DIFF_DV, 1), lambda b, h, i: (h, 0, 0)),
        ],
        out_specs=pl.BlockSpec((ATT_TQ, 128), lambda b, h, i: (b * nq + i, h)),
        out_shape=jax.ShapeDtypeStruct((n, DIFF_HEADS * DIFF_DV), BF16),
        scratch_shapes=[pltpu.VMEM((2, 1, ATT_TQ), F32),
                        pltpu.VMEM((2, ATT_VROWS, ATT_TQ), F32),
                        pltpu.VMEM((2, ATT_TK // ATT_SUB, ATT_SUB, ATT_TQ), F32)],
        compiler_params=_params("parallel", "parallel", "arbitrary"),
        name="diff_flash",
    )(lam_tab, qt, k, vt, g_norm.reshape(DIFF_HEADS, DIFF_DV, 1))


def _compare_exchange(x, i, j):
    hi, lo = jnp.maximum(x[i], x[j]), jnp.minimum(x[i], x[j])
    x[i], x[j] = hi, lo


def _bitonic_merge_desc(x):
    n = len(x)
    j = n // 2
    while j >= 1:
        for i in range(n):
            if i & j == 0:
                _compare_exchange(x, i, i | j)
        j //= 2


def _bitonic_sort_desc(x):
    n = len(x)
    k = 2
    while k <= n:
        j = k // 2
        while j >= 1:
            for i in range(n):
                l = i ^ j
                if l > i:
                    if i & k == 0:
                        _compare_exchange(x, i, l)
                    else:
                        _compare_exchange(x, l, i)
            j //= 2
        k *= 2


def _top16_per_column(slabs):
    x = list(slabs)
    n = len(x)
    _bitonic_sort_desc(x)
    for shift in (4, 2, 1):
        partner = [pltpu.roll(v, shift, 0) for v in x]
        x = [jnp.maximum(x[i], partner[n - 1 - i]) for i in range(n)]
        _bitonic_merge_desc(x)
    return [v[0:1, :] for v in x]


def _peer_select_kernel(q_ref, keys_ref, e1_ref, d1_ref, e2_ref, c2_ref):
    k_top = PEER_TOPK
    q = q_ref[...].astype(BF16)
    keys = keys_ref[0].astype(BF16)
    s1 = lax.dot_general(keys[0], q[:, :128], NT_DIMS, preferred_element_type=F32)
    s2 = lax.dot_general(keys[1], q[:, 128:], NT_DIMS, preferred_element_type=F32)
    a = _top16_per_column([s1[8 * g:8 * g + 8, :] for g in range(PEER_NKEYS // 8)])
    b = _top16_per_column([s2[8 * g:8 * g + 8, :] for g in range(PEER_NKEYS // 8)])
    pairs = [(i, j) for i in range(k_top) for j in range(k_top) if (i + 1) * (j + 1) <= k_top]
    cands = [a[i] + b[j] for i, j in pairs]
    neg = jnp.full_like(a[0], -jnp.inf)
    cands += [neg] * (4 * k_top - len(cands))
    slabs = [jnp.concatenate(cands[4 * g:4 * g + 4] + [neg] * 4, axis=0) for g in range(k_top)]
    best = _top16_per_column(slabs)
    top = best[0]
    z = jnp.exp(best[0] - top)
    for r in range(1, k_top):
        z = z + jnp.exp(best[r] - top)
    inv_z = 1.0 / z
    kth = best[k_top - 1]
    d1 = jnp.full_like(s1, float(k_top + 1))
    for i in range(k_top):
        n_i = sum(jnp.where(a[i] + b[j] >= kth, 1.0, 0.0) for (ii, j) in pairs if ii == i)
        d1 = jnp.where(s1 == a[i], float(k_top + 1) - n_i, d1)
    c2 = jnp.zeros_like(s2)
    for r in range(k_top):
        c2 = c2 + jnp.where(s2 >= b[r], 1.0, 0.0)
    e1_ref[0] = jnp.where(s1 >= a[k_top - 1], jnp.exp(s1 - a[0]), 0.0)
    d1_ref[0] = d1
    e2_ref[0] = jnp.where(c2 > 0.0, jnp.exp(s2 - b[0]) * inv_z, 0.0).astype(BF16)
    c2_ref[0] = c2.astype(BF16)


def peer_select(qry, sub_keys, *, tt):
    n = qry.shape[0]
    hh = PEER_HEADS
    spec = pl.BlockSpec((1, PEER_NKEYS, tt), lambda i, h: (h, 0, i))
    shape = lambda dt: jax.ShapeDtypeStruct((hh, PEER_NKEYS, n), dt)
    return pl.pallas_call(
        _peer_select_kernel,
        grid=(n // tt, hh),
        in_specs=[pl.BlockSpec((tt, 256), lambda i, h: (i, h)),
                  pl.BlockSpec((1, 2, PEER_NKEYS, 128), lambda i, h: (h, 0, 0, 0))],
        out_specs=[spec, spec, spec, spec],
        out_shape=[shape(F32), shape(F32), shape(BF16), shape(BF16)],
        compiler_params=_params("parallel", "arbitrary"),
        name="peer_select",
    )(qry, sub_keys)


PEER_TT = 512
PEER_ET = 1024
PEER_MIX = 256
BF16_ROWS = 16


def _peer_dense_kernel(xt_ref, u_ref, vt_ref, e1_ref, d1_ref, e2_ref, c2_ref, o_ref,
                       acts_ref, g_ref):
    @pl.when(pl.program_id(1) == 0)
    def _():
        o_ref[...] = jnp.zeros_like(o_ref)

    xt = xt_ref[...]
    n_rows = PEER_ET // PEER_NKEYS

    def activations(r):
        sl = slice(r * PEER_NKEYS, (r + 1) * PEER_NKEYS)
        acts_ref[sl, :] = jnp.dot(u_ref[sl, :], xt, preferred_element_type=F32)

    def gates(r):
        tiles = (PEER_NKEYS // BF16_ROWS, BF16_ROWS, PEER_TT)
        w = None
        for h in range(PEER_HEADS):
            e1 = jnp.broadcast_to(e1_ref[h, r:r + 1, :], tiles[1:]).astype(BF16)
            d1 = jnp.broadcast_to(d1_ref[h, r:r + 1, :], tiles[1:]).astype(BF16)
            e2 = e2_ref[h].reshape(tiles)
            c2 = c2_ref[h].reshape(tiles)
            p = e1[None] * e2
            c = jnp.where(c2 >= jnp.broadcast_to(d1[None], tiles), p, jnp.zeros_like(p))
            w = c if w is None else w + c
        return w.reshape(PEER_NKEYS, PEER_TT)

    def times_gelu(r, w):
        sl = slice(r * PEER_NKEYS, (r + 1) * PEER_NKEYS)
        a = acts_ref[sl, :]
        gelu = 0.5 * a * (1.0 + lax.erf(a * (2.0 ** -0.5)))
        g_ref[sl, :] = w * gelu.astype(BF16)

    def mix(q):
        sl = slice(q * PEER_MIX, (q + 1) * PEER_MIX)
        o_ref[...] += jnp.dot(vt_ref[:, sl], g_ref[sl, :], preferred_element_type=F32)

    per_mix = PEER_MIX // PEER_NKEYS
    activations(0)
    w = gates(0)
    activations(1)
    for r in range(n_rows):
        if r + 2 < n_rows:
            activations(r + 2)
        times_gelu(r, w)
        if r + 1 < n_rows:
            w = gates(r + 1)
        if (r + 1) % per_mix == 0:
            mix(r // per_mix)


def peer_dense(xn_t, u, v_t, e1, d1, e2, c2):
    d, n = xn_t.shape
    ne = u.shape[0]
    rows = PEER_ET // PEER_NKEYS
    per_row = pl.BlockSpec((PEER_HEADS, rows, PEER_TT), lambda i, j: (0, j, i))
    per_col = pl.BlockSpec((PEER_HEADS, PEER_NKEYS, PEER_TT), lambda i, j: (0, 0, i))
    return pl.pallas_call(
        _peer_dense_kernel,
        grid=(n // PEER_TT, ne // PEER_ET),
        in_specs=[pl.BlockSpec((d, PEER_TT), lambda i, j: (0, i)),
                  pl.BlockSpec((PEER_ET, d), lambda i, j: (j, 0)),
                  pl.BlockSpec((d, PEER_ET), lambda i, j: (0, j)),
                  per_row, per_row, per_col, per_col],
        out_specs=pl.BlockSpec((d, PEER_TT), lambda i, j: (0, i)),
        out_shape=jax.ShapeDtypeStruct((d, n), F32),
        scratch_shapes=[pltpu.VMEM((PEER_ET, PEER_TT), F32),
                        pltpu.VMEM((PEER_ET, PEER_TT), BF16)],
        compiler_params=_params("parallel", "arbitrary"),
        name="peer_dense",
    )(xn_t, u, v_t, e1, d1, e2, c2)


def _add_t_kernel(h_ref, ft_ref, g_ref, o_ref, *, final_norm):
    y = h_ref[...] + ft_ref[...].T
    if final_norm:
        y = y * lax.rsqrt(jnp.mean(y * y, axis=-1, keepdims=True) + RMS_EPS) * g_ref[...]
    o_ref[...] = y


def add_transposed(h, ff_t, g, *, tm, final_norm):
    n, d = h.shape
    return pl.pallas_call(
        functools.partial(_add_t_kernel, final_norm=final_norm),
        grid=(n // tm,),
        in_specs=[pl.BlockSpec((tm, d), lambda i: (i, 0)),
                  pl.BlockSpec((d, tm), lambda i: (0, i)),
                  pl.BlockSpec((1, d), lambda i: (0, 0))],
        out_specs=pl.BlockSpec((tm, d), lambda i: (i, 0)),
        out_shape=jax.ShapeDtypeStruct((n, d), F32),
        compiler_params=_params("parallel"),
        name="add_t_norm" if final_norm else "add_t",
    )(h, ff_t, g.reshape(1, d))


def _peer_layer(h, g_ffn, w_query, sub_keys, expert_u, expert_v, g_after, final_norm):
    qry, xn = rms_matmul(h, g_ffn, w_query.astype(BF16), tm=512, tn=1024, emit_xn=True)
    e1, d1, e2, c2 = peer_select(qry, sub_keys, tt=512)
    ff_t = peer_dense(xn.T, expert_u.astype(BF16), expert_v.astype(BF16).T, e1, d1, e2, c2)
    return add_transposed(h, ff_t, g_after, tm=512, final_norm=final_norm)


def _gla_layer(h, g_mix, w_in, w_gate_up, b_gate, g_norm, w_out, *, batch, seq):
    pad = GLA_IN_PAD - w_in.shape[1]
    w_in_pad = jnp.pad(w_in, ((0, 0), (0, pad))).astype(BF16)
    w_gate_pad = jnp.pad(w_gate_up, ((0, 128 - GLA_RANK), (0, 0)))
    proj = rms_matmul(h, g_mix, w_in_pad, tm=512, tn=640)
    o = gla_core(proj, w_gate_pad, b_gate, g_norm, batch=batch, seq=seq)
    return matmul_res(o, w_out.astype(BF16), h, tm=512, tn=1024)


def _diff_layer(h, positions, g_mix, w_in, lam_vecs, g_norm, w_out, lambda_init, *, batch, seq):
    proj = rms_matmul(h, g_mix, w_in.astype(BF16), tm=512, tn=1024)
    qt, k, vt = rope_prep(proj, positions.reshape(-1), tm=512)
    o = diff_flash(qt, k, vt, lam_vecs, g_norm, batch=batch, seq=seq, lambda_init=lambda_init)
    return matmul_res(o, w_out.astype(BF16), h, tm=512, tn=1024)


def kernel(x, positions, norm_mix, norm_ffn, norm_final, gla_w_in, gla_w_gate_up, gla_b_gate, gla_norm, gla_w_out, diff_w_in, diff_lambda_q1, diff_lambda_k1, diff_lambda_q2, diff_lambda_k2, diff_norm, diff_w_out, peer_w_query, peer_sub_keys, peer_u, peer_v):
    batch, seq, d = x.shape
    depth = norm_mix.shape[0]
    h = x.reshape(batch * seq, d)
    for i in range(depth):
        j = i // 2
        if i % 2 == 0:
            h = _gla_layer(h, norm_mix[i], gla_w_in[j], gla_w_gate_up[j], gla_b_gate[j],
                           gla_norm[j], gla_w_out[j], batch=batch, seq=seq)
        else:
            lambda_init = 0.8 - 0.6 * math.exp(-0.3 * i)
            lam_vecs = jnp.stack([diff_lambda_q1[j], diff_lambda_k1[j],
                                  diff_lambda_q2[j], diff_lambda_k2[j]])
            h = _diff_layer(h, positions, norm_mix[i], diff_w_in[j], lam_vecs, diff_norm[j],
                            diff_w_out[j], lambda_init, batch=batch, seq=seq)
        last = i == depth - 1
        h = _peer_layer(h, norm_ffn[i], peer_w_query[i], peer_sub_keys[i], peer_u[i], peer_v[i],
                        norm_final, final_norm=last)
    return h.reshape(batch, seq, d)
```

```python
import functools
import math

import jax
import jax.numpy as jnp
from jax import lax
from jax.experimental import pallas as pl
from jax.experimental.pallas import tpu as pltpu

D_MODEL = 1024
RMS_EPS = 1e-6

GLA_HEADS = 4
GLA_DK = 128
GLA_DV = 256
GLA_RANK = 16
GLA_TAU = 16.0
GLA_CHUNK = 64
GLA_QK = GLA_HEADS * GLA_DK
GLA_V = GLA_HEADS * GLA_DV
GLA_IN_PAD = 2 * GLA_QK + 2 * GLA_V + 128

DIFF_HEADS = 8
DIFF_DH = 64
DIFF_DV = 128
ROPE_DIMS = 16
ROPE_THETA = 500000.0
LOG2E = 1.4426950408889634
ATT_TQ = 1024
ATT_TK = 1024
ATT_ONES = 16
ATT_VROWS = DIFF_DV + ATT_ONES
ATT_SUB = 256

PEER_HEADS = 8
PEER_NKEYS = 128
PEER_TOPK = 16

V7X_VMEM_LIMIT_BYTES = 48 * 1024 * 1024
BF16 = jnp.bfloat16
F32 = jnp.float32
NT_DIMS = (((1,), (1,)), ((), ()))
TN_DIMS = (((0,), (0,)), ((), ()))


def _params(*sem):
    return pltpu.CompilerParams(dimension_semantics=sem,
                                vmem_limit_bytes=V7X_VMEM_LIMIT_BYTES)


def _rms_matmul_kernel(x_ref, g_ref, w_ref, o_ref, *rest, emit_xn):
    if emit_xn:
        xn_out_ref, xn_ref = rest
    else:
        (xn_ref,) = rest

    @pl.when(pl.program_id(1) == 0)
    def _():
        x = x_ref[...]
        y = x * lax.rsqrt(jnp.mean(x * x, axis=-1, keepdims=True) + RMS_EPS)
        xn_ref[...] = (y * g_ref[...]).astype(BF16)

    if emit_xn:
        xn_out_ref[...] = xn_ref[...]
    o_ref[...] = jnp.dot(xn_ref[...], w_ref[...], preferred_element_type=F32)


def rms_matmul(x, g, w, *, tm, tn, emit_xn=False):
    n, d = x.shape
    m = w.shape[1]
    out_shape = [jax.ShapeDtypeStruct((n, m), F32)]
    out_specs = [pl.BlockSpec((tm, tn), lambda i, j: (i, j))]
    if emit_xn:
        out_shape.append(jax.ShapeDtypeStruct((n, d), BF16))
        out_specs.append(pl.BlockSpec((tm, d), lambda i, j: (i, 0)))
    res = pl.pallas_call(
        functools.partial(_rms_matmul_kernel, emit_xn=emit_xn),
        grid=(n // tm, m // tn),
        in_specs=[pl.BlockSpec((tm, d), lambda i, j: (i, 0)),
                  pl.BlockSpec((1, d), lambda i, j: (0, 0)),
                  pl.BlockSpec((d, tn), lambda i, j: (0, j))],
        out_specs=out_specs,
        out_shape=out_shape,
        scratch_shapes=[pltpu.VMEM((tm, d), BF16)],
        compiler_params=_params("parallel", "arbitrary"),
        name="rms_matmul_xn" if emit_xn else "rms_matmul",
    )(x, g.reshape(1, d), w)
    return res if emit_xn else res[0]


def _matmul_res_kernel(a_ref, w_ref, r_ref, o_ref):
    o_ref[...] = r_ref[...] + jnp.dot(a_ref[...], w_ref[...], preferred_element_type=F32)


def matmul_res(a, w, res, *, tm, tn):
    n, k = a.shape
    m = w.shape[1]
    return pl.pallas_call(
        _matmul_res_kernel,
        grid=(n // tm, m // tn),
        in_specs=[pl.BlockSpec((tm, k), lambda i, j: (i, 0)),
                  pl.BlockSpec((k, tn), lambda i, j: (0, j)),
                  pl.BlockSpec((tm, tn), lambda i, j: (i, j))],
        out_specs=pl.BlockSpec((tm, tn), lambda i, j: (i, j)),
        out_shape=jax.ShapeDtypeStruct((n, m), F32),
        compiler_params=_params("parallel", "arbitrary"),
        name="matmul_res",
    )(a, w, res)


GLA_STEP = 512


def _gla_kernel(q_ref, k_ref, v_ref, r_ref, glr_ref, wg_ref, bg_ref, gn_ref,
                o_ref, state_ref):
    @pl.when(pl.program_id(1) == 0)
    def _():
        state_ref[...] = jnp.zeros_like(state_ref)

    c = GLA_CHUNK
    wg = wg_ref[...].astype(BF16)
    bg = bg_ref[...]
    row = lax.broadcasted_iota(jnp.int32, (c, c), 0)
    col = lax.broadcasted_iota(jnp.int32, (c, c), 1)
    causal = col <= row
    tril = causal.astype(BF16)

    def chunk(ci, carry):
        sl = pl.ds(pl.multiple_of(ci * c, c), c)
        q_all = q_ref[sl, :] * (GLA_DK ** -0.5)
        k_all = k_ref[sl, :]
        v_all = v_ref[sl, :].astype(BF16)
        r_all = r_ref[sl, :]
        z = jnp.dot(glr_ref[sl, :].astype(BF16), wg, preferred_element_type=F32) + bg
        glog = jax.nn.log_sigmoid(z) / GLA_TAU
        g_hi = glog.astype(BF16)
        g_lo = (glog - g_hi.astype(F32)).astype(BF16)
        cum_all = (jnp.dot(tril, g_hi, preferred_element_type=F32)
                   + jnp.dot(tril, g_lo, preferred_element_type=F32))
        outs = []
        for h in range(GLA_HEADS):
            ks = slice(h * GLA_DK, (h + 1) * GLA_DK)
            vs = slice(h * GLA_DV, (h + 1) * GLA_DV)
            q, k, v, cum = q_all[:, ks], k_all[:, ks], v_all[:, vs], cum_all[:, ks]
            last = cum[c - 1:c, :]
            mid = cum[c // 2 - 1:c // 2, :]
            qe = (q * jnp.exp(cum - mid)).astype(BF16)
            ke = (k * jnp.exp(mid - cum)).astype(BF16)
            a = lax.dot_general(qe, ke, NT_DIMS, preferred_element_type=F32)
            a = jnp.where(causal, a, 0.0).astype(BF16)
            o = jnp.dot(a, v, preferred_element_type=F32)
            st = state_ref[h]
            qc = (q * jnp.exp(cum)).astype(BF16)
            o = o + lax.dot_general(qc, st.astype(BF16), NT_DIMS, preferred_element_type=F32)
            kd = (k * jnp.exp(last - cum)).astype(BF16)
            kv_t = lax.dot_general(v, kd, TN_DIMS, preferred_element_type=F32)
            state_ref[h] = st * jnp.exp(last) + kv_t
            y = o * lax.rsqrt(jnp.mean(o * o, axis=-1, keepdims=True) + RMS_EPS) * gn_ref[h]
            r = r_all[:, vs]
            outs.append((y * (r * jax.nn.sigmoid(r))).astype(o_ref.dtype))
        o_ref[sl, :] = jnp.concatenate(outs, axis=1)
        return carry

    lax.fori_loop(0, GLA_STEP // c, chunk, 0)


def gla_core(proj, w_gate_up_pad, b_gate, g_norm, *, batch, seq):
    n = proj.shape[0]
    steps = seq // GLA_STEP

    def rows(b, s):
        return b * steps + s

    return pl.pallas_call(
        _gla_kernel,
        grid=(batch, steps),
        in_specs=[
            pl.BlockSpec((GLA_STEP, GLA_QK), lambda b, s: (rows(b, s), 0)),
            pl.BlockSpec((GLA_STEP, GLA_QK), lambda b, s: (rows(b, s), 1)),
            pl.BlockSpec((GLA_STEP, GLA_V), lambda b, s: (rows(b, s), 1)),
            pl.BlockSpec((GLA_STEP, GLA_V), lambda b, s: (rows(b, s), 2)),
            pl.BlockSpec((GLA_STEP, 128), lambda b, s: (rows(b, s), (2 * GLA_QK + 2 * GLA_V) // 128)),
            pl.BlockSpec((128, GLA_QK), lambda b, s: (0, 0)),
            pl.BlockSpec((1, GLA_QK), lambda b, s: (0, 0)),
            pl.BlockSpec((GLA_HEADS, 1, GLA_DV), lambda b, s: (0, 0, 0)),
        ],
        out_specs=pl.BlockSpec((GLA_STEP, GLA_V), lambda b, s: (rows(b, s), 0)),
        out_shape=jax.ShapeDtypeStruct((n, GLA_V), BF16),
        scratch_shapes=[pltpu.VMEM((GLA_HEADS, GLA_DV, GLA_DK), F32)],
        compiler_params=_params("parallel", "arbitrary"),
        name="gla_core",
    )(proj, proj, proj, proj, proj, w_gate_up_pad, b_gate.reshape(1, GLA_QK),
      g_norm.reshape(GLA_HEADS, 1, GLA_DV))


def _rope_kernel(pos_ref, freq_ref, q_ref, k_ref, v_ref, qo_ref, ko_ref, vo_ref):
    ang = pos_ref[...].astype(F32) * freq_ref[0:1, :]
    cos = jnp.cos(ang)
    sin = jnp.sin(ang)
    sgn_lo = freq_ref[1:2, :]
    sgn_hi = freq_ref[2:3, :]
    half = ROPE_DIMS // 2

    def rot(t, scale):
        outs = []
        for h in range(DIFF_HEADS):
            x = t[:, h * 128:(h + 1) * 128]
            partner = (sgn_lo * pltpu.roll(x, 128 - half, 1)
                       + sgn_hi * pltpu.roll(x, half, 1))
            outs.append((x * cos + partner * sin) * scale)
        return jnp.concatenate(outs, axis=1)

    qo_ref[...] = rot(q_ref[...], DIFF_DH ** -0.5 * LOG2E).T.astype(BF16)
    ko_ref[...] = rot(k_ref[...], 1.0).astype(BF16)
    vt = v_ref[...].T
    ones = jnp.ones((ATT_ONES, vt.shape[1]), F32)
    vo_ref[...] = jnp.concatenate(
        [piece for h in range(DIFF_HEADS)
         for piece in (vt[h * DIFF_DV:(h + 1) * DIFF_DV], ones)], axis=0).astype(BF16)


def rope_prep(proj, positions, *, tm):
    n = proj.shape[0]
    w = DIFF_HEADS * 128
    lane = jnp.arange(128) % DIFF_DH
    inv_freq = ROPE_THETA ** (-jnp.arange(0, ROPE_DIMS, 2, dtype=F32) / ROPE_DIMS)
    half = ROPE_DIMS // 2
    freq = jnp.where(lane < ROPE_DIMS, inv_freq[lane % half], 0.0)
    sgn_lo = jnp.where(lane < half, -1.0, 0.0)
    sgn_hi = jnp.where((lane >= half) & (lane < ROPE_DIMS), 1.0, 0.0)
    table = jnp.zeros((8, 128), F32).at[0].set(freq).at[1].set(sgn_lo).at[2].set(sgn_hi)
    spec_in = lambda c: pl.BlockSpec((tm, w), lambda i: (i, c))
    row_major = jax.ShapeDtypeStruct((n, w), BF16)
    transposed = jax.ShapeDtypeStruct((w, n), BF16)
    spec_t = pl.BlockSpec((w, tm), lambda i: (0, i))
    wv = DIFF_HEADS * ATT_VROWS
    return pl.pallas_call(
        _rope_kernel,
        grid=(n // tm,),
        in_specs=[pl.BlockSpec((tm, 1), lambda i: (i, 0)),
                  pl.BlockSpec((8, 128), lambda i: (0, 0)),
                  spec_in(0), spec_in(1), spec_in(2)],
        out_specs=[spec_t, pl.BlockSpec((tm, w), lambda i: (i, 0)),
                   pl.BlockSpec((wv, tm), lambda i: (0, i))],
        out_shape=[transposed, row_major, jax.ShapeDtypeStruct((wv, n), BF16)],
        compiler_params=_params("parallel"),
        name="rope_prep",
    )(positions.reshape(n, 1), table, proj, proj, proj)


def _flash_kernel(lam_ref, qt_ref, k_ref, vt_ref, gn_ref, o_ref,
                  m_ref, acc_ref, s_ref, *, lambda_init):
    qi = pl.program_id(2)
    assert ATT_TQ == ATT_TK
    m_ref[...] = jnp.full_like(m_ref, -jnp.inf)
    acc_ref[...] = jnp.zeros_like(acc_ref)
    qt = qt_ref[...]
    lane = lax.broadcasted_iota(jnp.int32, (ATT_SUB, 128), 1)
    n_sub = ATT_TK // ATT_SUB

    def accumulate(kj, masked):
        base = pl.multiple_of(kj * ATT_TK, ATT_TK)

        def first_query(c):
            return c * ATT_SUB if masked else 0

        def scores(c):
            k = k_ref[pl.ds(base + c * ATT_SUB, ATT_SUB), :]
            q0 = first_query(c)
            for m in range(2):
                in_map = (lane < DIFF_DH) if m == 0 else (lane >= DIFF_DH)
                km = jnp.where(in_map, k, jnp.zeros_like(k))
                s_ref[m, c, :, q0:] = jnp.dot(km, qt[:, q0:],
                                              preferred_element_type=F32)

        def softmax_pv(c):
            vt = vt_ref[:, pl.ds(base + c * ATT_SUB, ATT_SUB)]
            q0 = first_query(c)
            for m in range(2):
                s = s_ref[m, c, :, q0:]
                if masked:
                    key = lax.broadcasted_iota(jnp.int32, s.shape, 0)
                    qry = lax.broadcasted_iota(jnp.int32, s.shape, 1)
                    s = jnp.where(key <= qry, s, -jnp.inf)
                m_old = m_ref[m, :, q0:]
                m_new = jnp.maximum(m_old, jnp.max(s, axis=0, keepdims=True))
                p = jnp.exp2((s - m_new).astype(BF16))
                alpha = jnp.exp2(m_old - m_new)
                acc_ref[m, :, q0:] = (alpha * acc_ref[m, :, q0:]
                                      + jnp.dot(vt, p, preferred_element_type=F32))
                m_ref[m, :, q0:] = m_new

        scores(0)
        for c in range(n_sub):
            if c + 1 < n_sub:
                scores(c + 1)
            softmax_pv(c)

    def full_block(kj, carry):
        accumulate(kj, False)
        return carry

    lax.fori_loop(0, qi, full_block, 0)
    accumulate(qi, True)

    lv = lam_ref[...]
    lam = (jnp.exp(jnp.sum(lv[0:1] * lv[1:2], axis=1, keepdims=True))
           - jnp.exp(jnp.sum(lv[2:3] * lv[3:4], axis=1, keepdims=True)) + lambda_init)
    a0 = acc_ref[0]
    a1 = acc_ref[1]
    o = (a0[:DIFF_DV] / a0[DIFF_DV:DIFF_DV + 1]
         - lam * (a1[:DIFF_DV] / a1[DIFF_DV:DIFF_DV + 1]))
    y = o * lax.rsqrt(jnp.mean(o * o, axis=0, keepdims=True) + RMS_EPS)
    y = y * (gn_ref[0] * (1.0 - lambda_init))
    o_ref[...] = y.T.astype(o_ref.dtype)


def diff_flash(qt, k, vt, lam_vecs, g_norm, *, batch, seq, lambda_init):
    n = k.shape[0]
    nq = seq // ATT_TQ
    lam_tab = jnp.zeros((8, 128), F32).at[0:4, 0:DIFF_DH].set(lam_vecs)
    return pl.pallas_call(
        functools.partial(_flash_kernel, lambda_init=lambda_init),
        grid=(batch, DIFF_HEADS, nq),
        in_specs=[
            pl.BlockSpec((8, 128), lambda b, h, i: (0, 0)),
            pl.BlockSpec((128, ATT_TQ), lambda b, h, i: (h, b * nq + i)),
            pl.BlockSpec((seq, 128), lambda b, h, i: (b, h)),
            pl.BlockSpec((ATT_VROWS, seq), lambda b, h, i: (h, b)),
            pl.BlockSpec((1, DIFF_DV, 1), lambda b, h, i: (h, 0, 0)),
        ],
        out_specs=pl.BlockSpec((ATT_TQ, 128), lambda b, h, i: (b * nq + i, h)),
        out_shape=jax.ShapeDtypeStruct((n, DIFF_HEADS * DIFF_DV), BF16),
        scratch_shapes=[pltpu.VMEM((2, 1, ATT_TQ), F32),
                        pltpu.VMEM((2, ATT_VROWS, ATT_TQ), F32),
                        pltpu.VMEM((2, ATT_TK // ATT_SUB, ATT_SUB, ATT_TQ), F32)],
        compiler_params=_params("parallel", "parallel", "arbitrary"),
        name="diff_flash",
    )(lam_tab, qt, k, vt, g_norm.reshape(DIFF_HEADS, DIFF_DV, 1))


def _compare_exchange(x, i, j):
    hi, lo = jnp.maximum(x[i], x[j]), jnp.minimum(x[i], x[j])
    x[i], x[j] = hi, lo


def _bitonic_merge_desc(x):
    n = len(x)
    j = n // 2
    while j >= 1:
        for i in range(n):
            if i & j == 0:
                _compare_exchange(x, i, i | j)
        j //= 2


def _bitonic_sort_desc(x):
    n = len(x)
    k = 2
    while k <= n:
        j = k // 2
        while j >= 1:
            for i in range(n):
                l = i ^ j
                if l > i:
                    if i & k == 0:
                        _compare_exchange(x, i, l)
                    else:
                        _compare_exchange(x, l, i)
            j //= 2
        k *= 2


def _top16_per_column(slabs):
    x = list(slabs)
    n = len(x)
    _bitonic_sort_desc(x)
    for shift in (4, 2, 1):
        partner = [pltpu.roll(v, shift, 0) for v in x]
        x = [jnp.maximum(x[i], partner[n - 1 - i]) for i in range(n)]
        _bitonic_merge_desc(x)
    return [v[0:1, :] for v in x]


def _peer_select_kernel(q_ref, keys_ref, e1_ref, d1_ref, e2_ref, c2_ref):
    k_top = PEER_TOPK
    q = q_ref[...].astype(BF16)
    keys = keys_ref[0].astype(BF16)
    s1 = lax.dot_general(keys[0], q[:, :128], NT_DIMS, preferred_element_type=F32)
    s2 = lax.dot_general(keys[1], q[:, 128:], NT_DIMS, preferred_element_type=F32)
    a = _top16_per_column([s1[8 * g:8 * g + 8, :] for g in range(PEER_NKEYS // 8)])
    b = _top16_per_column([s2[8 * g:8 * g + 8, :] for g in range(PEER_NKEYS // 8)])
    pairs = [(i, j) for i in range(k_top) for j in range(k_top) if (i + 1) * (j + 1) <= k_top]
    cands = [a[i] + b[j] for i, j in pairs]
    neg = jnp.full_like(a[0], -jnp.inf)
    cands += [neg] * (4 * k_top - len(cands))
    slabs = [jnp.concatenate(cands[4 * g:4 * g + 4] + [neg] * 4, axis=0) for g in range(k_top)]
    best = _top16_per_column(slabs)
    top = best[0]
    z = jnp.exp(best[0] - top)
    for r in range(1, k_top):
        z = z + jnp.exp(best[r] - top)
    inv_z = 1.0 / z
    kth = best[k_top - 1]
    d1 = jnp.full_like(s1, float(k_top + 1))
    for i in range(k_top):
        n_i = sum(jnp.where(a[i] + b[j] >= kth, 1.0, 0.0) for (ii, j) in pairs if ii == i)
        d1 = jnp.where(s1 == a[i], float(k_top + 1) - n_i, d1)
    c2 = jnp.zeros_like(s2)
    for r in range(k_top):
        c2 = c2 + jnp.where(s2 >= b[r], 1.0, 0.0)
    e1_ref[0] = jnp.where(s1 >= a[k_top - 1], jnp.exp(s1 - a[0]), 0.0)
    d1_ref[0] = d1
    e2_ref[0] = jnp.where(c2 > 0.0, jnp.exp(s2 - b[0]) * inv_z, 0.0).astype(BF16)
    c2_ref[0] = c2.astype(BF16)


def peer_select(qry, sub_keys, *, tt):
    n = qry.shape[0]
    hh = PEER_HEADS
    spec = pl.BlockSpec((1, PEER_NKEYS, tt), lambda i, h: (h, 0, i))
    shape = lambda dt: jax.ShapeDtypeStruct((hh, PEER_NKEYS, n), dt)
    return pl.pallas_call(
        _peer_select_kernel,
        grid=(n // tt, hh),
        in_specs=[pl.BlockSpec((tt, 256), lambda i, h: (i, h)),
                  pl.BlockSpec((1, 2, PEER_NKEYS, 128), lambda i, h: (h, 0, 0, 0))],
        out_specs=[spec, spec, spec, spec],
        out_shape=[shape(F32), shape(F32), shape(BF16), shape(BF16)],
        compiler_params=_params("parallel", "arbitrary"),
        name="peer_select",
    )(qry, sub_keys)


PEER_TT = 512
PEER_ET = 1024
PEER_MIX = 256
BF16_ROWS = 16


def _peer_dense_kernel(xt_ref, u_ref, vt_ref, e1_ref, d1_ref, e2_ref, c2_ref, o_ref,
                       acts_ref, g_ref):
    @pl.when(pl.program_id(1) == 0)
    def _():
        o_ref[...] = jnp.zeros_like(o_ref)

    xt = xt_ref[...]
    n_rows = PEER_ET // PEER_NKEYS

    def activations(r):
        sl = slice(r * PEER_NKEYS, (r + 1) * PEER_NKEYS)
        acts_ref[sl, :] = jnp.dot(u_ref[sl, :], xt, preferred_element_type=F32)

    def gates(r):
        tiles = (PEER_NKEYS // BF16_ROWS, BF16_ROWS, PEER_TT)
        w = None
        for h in range(PEER_HEADS):
            e1 = jnp.broadcast_to(e1_ref[h, r:r + 1, :], tiles[1:]).astype(BF16)
            d1 = jnp.broadcast_to(d1_ref[h, r:r + 1, :], tiles[1:]).astype(BF16)
            e2 = e2_ref[h].reshape(tiles)
            c2 = c2_ref[h].reshape(tiles)
            p = e1[None] * e2
            c = jnp.where(c2 >= jnp.broadcast_to(d1[None], tiles), p, jnp.zeros_like(p))
            w = c if w is None else w + c
        return w.reshape(PEER_NKEYS, PEER_TT)

    def times_gelu(r, w):
        sl = slice(r * PEER_NKEYS, (r + 1) * PEER_NKEYS)
        a = acts_ref[sl, :]
        gelu = 0.5 * a * (1.0 + lax.erf(a * (2.0 ** -0.5)))
        g_ref[sl, :] = w * gelu.astype(BF16)

    def mix(q):
        sl = slice(q * PEER_MIX, (q + 1) * PEER_MIX)
        o_ref[...] += jnp.dot(vt_ref[:, sl], g_ref[sl, :], preferred_element_type=F32)

    per_mix = PEER_MIX // PEER_NKEYS
    activations(0)
    w = gates(0)
    activations(1)
    for r in range(n_rows):
        if r + 2 < n_rows:
            activations(r + 2)
        times_gelu(r, w)
        if r + 1 < n_rows:
            w = gates(r + 1)
        if (r + 1) % per_mix == 0:
            mix(r // per_mix)


def peer_dense(xn_t, u, v_t, e1, d1, e2, c2):
    d, n = xn_t.shape
    ne = u.shape[0]
    rows = PEER_ET // PEER_NKEYS
    per_row = pl.BlockSpec((PEER_HEADS, rows, PEER_TT), lambda i, j: (0, j, i))
    per_col = pl.BlockSpec((PEER_HEADS, PEER_NKEYS, PEER_TT), lambda i, j: (0, 0, i))
    return pl.pallas_call(
        _peer_dense_kernel,
        grid=(n // PEER_TT, ne // PEER_ET),
        in_specs=[pl.BlockSpec((d, PEER_TT), lambda i, j: (0, i)),
                  pl.BlockSpec((PEER_ET, d), lambda i, j: (j, 0)),
                  pl.BlockSpec((d, PEER_ET), lambda i, j: (0, j)),
                  per_row, per_row, per_col, per_col],
        out_specs=pl.BlockSpec((d, PEER_TT), lambda i, j: (0, i)),
        out_shape=jax.ShapeDtypeStruct((d, n), F32),
        scratch_shapes=[pltpu.VMEM((PEER_ET, PEER_TT), F32),
                        pltpu.VMEM((PEER_ET, PEER_TT), BF16)],
        compiler_params=_params("parallel", "arbitrary"),
        name="peer_dense",
    )(xn_t, u, v_t, e1, d1, e2, c2)


def _add_t_kernel(h_ref, ft_ref, g_ref, o_ref, *, final_norm):
    y = h_ref[...] + ft_ref[...].T
    if final_norm:
        y = y * lax.rsqrt(jnp.mean(y * y, axis=-1, keepdims=True) + RMS_EPS) * g_ref[...]
    o_ref[...] = y


def add_transposed(h, ff_t, g, *, tm, final_norm):
    n, d = h.shape
    return pl.pallas_call(
        functools.partial(_add_t_kernel, final_norm=final_norm),
        grid=(n // tm,),
        in_specs=[pl.BlockSpec((tm, d), lambda i: (i, 0)),
                  pl.BlockSpec((d, tm), lambda i: (0, i)),
                  pl.BlockSpec((1, d), lambda i: (0, 0))],
        out_specs=pl.BlockSpec((tm, d), lambda i: (i, 0)),
        out_shape=jax.ShapeDtypeStruct((n, d), F32),
        compiler_params=_params("parallel"),
        name="add_t_norm" if final_norm else "add_t",
    )(h, ff_t, g.reshape(1, d))


def _peer_layer(h, g_ffn, w_query, sub_keys, expert_u, expert_v, g_after, final_norm):
    qry, xn = rms_matmul(h, g_ffn, w_query.astype(BF16), tm=1024, tn=1024, emit_xn=True)
    e1, d1, e2, c2 = peer_select(qry, sub_keys, tt=512)
    ff_t = peer_dense(xn.T, expert_u.astype(BF16), expert_v.astype(BF16).T, e1, d1, e2, c2)
    return add_transposed(h, ff_t, g_after, tm=512, final_norm=final_norm)


def _gla_layer(h, g_mix, w_in, w_gate_up, b_gate, g_norm, w_out, *, batch, seq):
    pad = GLA_IN_PAD - w_in.shape[1]
    w_in_pad = jnp.pad(w_in, ((0, 0), (0, pad))).astype(BF16)
    w_gate_pad = jnp.pad(w_gate_up, ((0, 128 - GLA_RANK), (0, 0)))
    proj = rms_matmul(h, g_mix, w_in_pad, tm=1024, tn=640)
    o = gla_core(proj, w_gate_pad, b_gate, g_norm, batch=batch, seq=seq)
    return matmul_res(o, w_out.astype(BF16), h, tm=512, tn=1024)


def _diff_layer(h, positions, g_mix, w_in, lam_vecs, g_norm, w_out, lambda_init, *, batch, seq):
    proj = rms_matmul(h, g_mix, w_in.astype(BF16), tm=1024, tn=1024)
    qt, k, vt = rope_prep(proj, positions.reshape(-1), tm=512)
    o = diff_flash(qt, k, vt, lam_vecs, g_norm, batch=batch, seq=seq, lambda_init=lambda_init)
    return matmul_res(o, w_out.astype(BF16), h, tm=512, tn=1024)


def kernel(x, positions, norm_mix, norm_ffn, norm_final, gla_w_in, gla_w_gate_up, gla_b_gate, gla_norm, gla_w_out, diff_w_in, diff_lambda_q1, diff_lambda_k1, diff_lambda_q2, diff_lambda_k2, diff_norm, diff_w_out, peer_w_query, peer_sub_keys, peer_u, peer_v):
    batch, seq, d = x.shape
    depth = norm_mix.shape[0]
    h = x.reshape(batch * seq, d)
    for i in range(depth):
        j = i // 2
        if i % 2 == 0:
            h = _gla_layer(h, norm_mix[i], gla_w_in[j], gla_w_gate_up[j], gla_b_gate[j],
                           gla_norm[j], gla_w_out[j], batch=batch, seq=seq)
        else:
            lambda_init = 0.8 - 0.6 * math.exp(-0.3 * i)
            lam_vecs = jnp.stack([diff_lambda_q1[j], diff_lambda_k1[j],
                                  diff_lambda_q2[j], diff_lambda_k2[j]])
            h = _diff_layer(h, positions, norm_mix[i], diff_w_in[j], lam_vecs, diff_norm[j],
                            diff_w_out[j], lambda_init, batch=batch, seq=seq)
        last = i == depth - 1
        h = _peer_layer(h, norm_ffn[i], peer_w_query[i], peer_sub_keys[i], peer_u[i], peer_v[i],
                        norm_final, final_norm=last)
    return h.reshape(batch, seq, d)
```

```python
import functools
import math

import jax
import jax.numpy as jnp
from jax import lax
from jax.experimental import pallas as pl
from jax.experimental.pallas import tpu as pltpu

D_MODEL = 1024
RMS_EPS = 1e-6

GLA_HEADS = 4
GLA_DK = 128
GLA_DV = 256
GLA_RANK = 16
GLA_TAU = 16.0
GLA_CHUNK = 64
GLA_QK = GLA_HEADS * GLA_DK
GLA_V = GLA_HEADS * GLA_DV
GLA_IN_PAD = 2 * GLA_QK + 2 * GLA_V + 128

DIFF_HEADS = 8
DIFF_DH = 64
DIFF_DV = 128
ROPE_DIMS = 16
ROPE_THETA = 500000.0
LOG2E = 1.4426950408889634
ATT_TQ = 1024
ATT_TK = 1024
ATT_ONES = 16
ATT_VROWS = DIFF_DV + ATT_ONES
ATT_SUB = 256

PEER_HEADS = 8
PEER_NKEYS = 128
PEER_TOPK = 16

V7X_VMEM_LIMIT_BYTES = 48 * 1024 * 1024
BF16 = jnp.bfloat16
F32 = jnp.float32
NT_DIMS = (((1,), (1,)), ((), ()))
TN_DIMS = (((0,), (0,)), ((), ()))


def _params(*sem):
    return pltpu.CompilerParams(dimension_semantics=sem,
                                vmem_limit_bytes=V7X_VMEM_LIMIT_BYTES)


def _rms_matmul_kernel(x_ref, g_ref, w_ref, o_ref, *rest, emit_xn):
    if emit_xn:
        xn_out_ref, xn_ref = rest
    else:
        (xn_ref,) = rest

    @pl.when(pl.program_id(1) == 0)
    def _():
        x = x_ref[...]
        y = x * lax.rsqrt(jnp.mean(x * x, axis=-1, keepdims=True) + RMS_EPS)
        xn_ref[...] = (y * g_ref[...]).astype(BF16)

    if emit_xn:
        xn_out_ref[...] = xn_ref[...]
    o_ref[...] = jnp.dot(xn_ref[...], w_ref[...], preferred_element_type=F32)


def rms_matmul(x, g, w, *, tm, tn, emit_xn=False):
    n, d = x.shape
    m = w.shape[1]
    out_shape = [jax.ShapeDtypeStruct((n, m), F32)]
    out_specs = [pl.BlockSpec((tm, tn), lambda i, j: (i, j))]
    if emit_xn:
        out_shape.append(jax.ShapeDtypeStruct((n, d), BF16))
        out_specs.append(pl.BlockSpec((tm, d), lambda i, j: (i, 0)))
    res = pl.pallas_call(
        functools.partial(_rms_matmul_kernel, emit_xn=emit_xn),
        grid=(n // tm, m // tn),
        in_specs=[pl.BlockSpec((tm, d), lambda i, j: (i, 0)),
                  pl.BlockSpec((1, d), lambda i, j: (0, 0)),
                  pl.BlockSpec((d, tn), lambda i, j: (0, j))],
        out_specs=out_specs,
        out_shape=out_shape,
        scratch_shapes=[pltpu.VMEM((tm, d), BF16)],
        compiler_params=_params("parallel", "arbitrary"),
        name="rms_matmul_xn" if emit_xn else "rms_matmul",
    )(x, g.reshape(1, d), w)
    return res if emit_xn else res[0]


def _matmul_res_kernel(a_ref, w_ref, r_ref, o_ref):
    o_ref[...] = r_ref[...] + jnp.dot(a_ref[...], w_ref[...], preferred_element_type=F32)


def matmul_res(a, w, res, *, tm, tn):
    n, k = a.shape
    m = w.shape[1]
    return pl.pallas_call(
        _matmul_res_kernel,
        grid=(n // tm, m // tn),
        in_specs=[pl.BlockSpec((tm, k), lambda i, j: (i, 0)),
                  pl.BlockSpec((k, tn), lambda i, j: (0, j)),
                  pl.BlockSpec((tm, tn), lambda i, j: (i, j))],
        out_specs=pl.BlockSpec((tm, tn), lambda i, j: (i, j)),
        out_shape=jax.ShapeDtypeStruct((n, m), F32),
        compiler_params=_params("parallel", "arbitrary"),
        name="matmul_res",
    )(a, w, res)


GLA_STEP = 512


def _gla_kernel(q_ref, k_ref, v_ref, r_ref, glr_ref, wg_ref, bg_ref, gn_ref,
                o_ref, state_ref):
    @pl.when(pl.program_id(1) == 0)
    def _():
        state_ref[...] = jnp.zeros_like(state_ref)

    c = GLA_CHUNK
    wg = wg_ref[...].astype(BF16)
    bg = bg_ref[...]
    row = lax.broadcasted_iota(jnp.int32, (c, c), 0)
    col = lax.broadcasted_iota(jnp.int32, (c, c), 1)
    causal = col <= row
    tril = causal.astype(BF16)

    def chunk(ci, carry):
        sl = pl.ds(pl.multiple_of(ci * c, c), c)
        q_all = q_ref[sl, :] * (GLA_DK ** -0.5)
        k_all = k_ref[sl, :]
        v_all = v_ref[sl, :].astype(BF16)
        r_all = r_ref[sl, :]
        z = jnp.dot(glr_ref[sl, :].astype(BF16), wg, preferred_element_type=F32) + bg
        glog = jax.nn.log_sigmoid(z) / GLA_TAU
        g_hi = glog.astype(BF16)
        g_lo = (glog - g_hi.astype(F32)).astype(BF16)
        cum_all = (jnp.dot(tril, g_hi, preferred_element_type=F32)
                   + jnp.dot(tril, g_lo, preferred_element_type=F32))
        outs = []
        for h in range(GLA_HEADS):
            ks = slice(h * GLA_DK, (h + 1) * GLA_DK)
            vs = slice(h * GLA_DV, (h + 1) * GLA_DV)
            q, k, v, cum = q_all[:, ks], k_all[:, ks], v_all[:, vs], cum_all[:, ks]
            last = cum[c - 1:c, :]
            mid = cum[c // 2 - 1:c // 2, :]
            qe = (q * jnp.exp(cum - mid)).astype(BF16)
            ke = (k * jnp.exp(mid - cum)).astype(BF16)
            a = lax.dot_general(qe, ke, NT_DIMS, preferred_element_type=F32)
            a = jnp.where(causal, a, 0.0).astype(BF16)
            o = jnp.dot(a, v, preferred_element_type=F32)
            st = state_ref[h]
            qc = (q * jnp.exp(cum)).astype(BF16)
            o = o + lax.dot_general(qc, st.astype(BF16), NT_DIMS, preferred_element_type=F32)
            kd = (k * jnp.exp(last - cum)).astype(BF16)
            kv_t = lax.dot_general(v, kd, TN_DIMS, preferred_element_type=F32)
            state_ref[h] = st * jnp.exp(last) + kv_t
            y = o * lax.rsqrt(jnp.mean(o * o, axis=-1, keepdims=True) + RMS_EPS) * gn_ref[h]
            r = r_all[:, vs]
            outs.append((y * (r * jax.nn.sigmoid(r))).astype(o_ref.dtype))
        o_ref[sl, :] = jnp.concatenate(outs, axis=1)
        return carry

    lax.fori_loop(0, GLA_STEP // c, chunk, 0)


def gla_core(proj, w_gate_up_pad, b_gate, g_norm, *, batch, seq):
    n = proj.shape[0]
    steps = seq // GLA_STEP

    def rows(b, s):
        return b * steps + s

    return pl.pallas_call(
        _gla_kernel,
        grid=(batch, steps),
        in_specs=[
            pl.BlockSpec((GLA_STEP, GLA_QK), lambda b, s: (rows(b, s), 0)),
            pl.BlockSpec((GLA_STEP, GLA_QK), lambda b, s: (rows(b, s), 1)),
            pl.BlockSpec((GLA_STEP, GLA_V), lambda b, s: (rows(b, s), 1)),
            pl.BlockSpec((GLA_STEP, GLA_V), lambda b, s: (rows(b, s), 2)),
            pl.BlockSpec((GLA_STEP, 128), lambda b, s: (rows(b, s), (2 * GLA_QK + 2 * GLA_V) // 128)),
            pl.BlockSpec((128, GLA_QK), lambda b, s: (0, 0)),
            pl.BlockSpec((1, GLA_QK), lambda b, s: (0, 0)),
            pl.BlockSpec((GLA_HEADS, 1, GLA_DV), lambda b, s: (0, 0, 0)),
        ],
        out_specs=pl.BlockSpec((GLA_STEP, GLA_V), lambda b, s: (rows(b, s), 0)),
        out_shape=jax.ShapeDtypeStruct((n, GLA_V), BF16),
        scratch_shapes=[pltpu.VMEM((GLA_HEADS, GLA_DV, GLA_DK), F32)],
        compiler_params=_params("parallel", "arbitrary"),
        name="gla_core",
    )(proj, proj, proj, proj, proj, w_gate_up_pad, b_gate.reshape(1, GLA_QK),
      g_norm.reshape(GLA_HEADS, 1, GLA_DV))


def _rope_kernel(pos_ref, freq_ref, q_ref, k_ref, v_ref, qo_ref, ko_ref, vo_ref):
    ang = pos_ref[...].astype(F32) * freq_ref[0:1, :]
    cos = jnp.cos(ang)
    sin = jnp.sin(ang)
    sgn_lo = freq_ref[1:2, :]
    sgn_hi = freq_ref[2:3, :]
    half = ROPE_DIMS // 2

    def rot(t, scale):
        outs = []
        for h in range(DIFF_HEADS):
            x = t[:, h * 128:(h + 1) * 128]
            partner = (sgn_lo * pltpu.roll(x, 128 - half, 1)
                       + sgn_hi * pltpu.roll(x, half, 1))
            outs.append((x * cos + partner * sin) * scale)
        return jnp.concatenate(outs, axis=1)

    qo_ref[...] = rot(q_ref[...], DIFF_DH ** -0.5 * LOG2E).T.astype(BF16)
    ko_ref[...] = rot(k_ref[...], 1.0).astype(BF16)
    vt = v_ref[...].T
    ones = jnp.ones((ATT_ONES, vt.shape[1]), F32)
    vo_ref[...] = jnp.concatenate(
        [piece for h in range(DIFF_HEADS)
         for piece in (vt[h * DIFF_DV:(h + 1) * DIFF_DV], ones)], axis=0).astype(BF16)


def rope_prep(proj, positions, *, tm):
    n = proj.shape[0]
    w = DIFF_HEADS * 128
    lane = jnp.arange(128) % DIFF_DH
    inv_freq = ROPE_THETA ** (-jnp.arange(0, ROPE_DIMS, 2, dtype=F32) / ROPE_DIMS)
    half = ROPE_DIMS // 2
    freq = jnp.where(lane < ROPE_DIMS, inv_freq[lane % half], 0.0)
    sgn_lo = jnp.where(lane < half, -1.0, 0.0)
    sgn_hi = jnp.where((lane >= half) & (lane < ROPE_DIMS), 1.0, 0.0)
    table = jnp.zeros((8, 128), F32).at[0].set(freq).at[1].set(sgn_lo).at[2].set(sgn_hi)
    spec_in = lambda c: pl.BlockSpec((tm, w), lambda i: (i, c))
    row_major = jax.ShapeDtypeStruct((n, w), BF16)
    transposed = jax.ShapeDtypeStruct((w, n), BF16)
    spec_t = pl.BlockSpec((w, tm), lambda i: (0, i))
    wv = DIFF_HEADS * ATT_VROWS
    return pl.pallas_call(
        _rope_kernel,
        grid=(n // tm,),
        in_specs=[pl.BlockSpec((tm, 1), lambda i: (i, 0)),
                  pl.BlockSpec((8, 128), lambda i: (0, 0)),
                  spec_in(0), spec_in(1), spec_in(2)],
        out_specs=[spec_t, pl.BlockSpec((tm, w), lambda i: (i, 0)),
                   pl.BlockSpec((wv, tm), lambda i: (0, i))],
        out_shape=[transposed, row_major, jax.ShapeDtypeStruct((wv, n), BF16)],
        compiler_params=_params("parallel"),
        name="rope_prep",
    )(positions.reshape(n, 1), table, proj, proj, proj)


def _flash_kernel(lam_ref, qt_ref, k_ref, vt_ref, gn_ref, o_ref,
                  m_ref, acc_ref, s_ref, *, lambda_init):
    qi = pl.program_id(2)
    assert ATT_TQ == ATT_TK
    m_ref[...] = jnp.full_like(m_ref, -jnp.inf)
    acc_ref[...] = jnp.zeros_like(acc_ref)
    qt = qt_ref[...]
    lane = lax.broadcasted_iota(jnp.int32, (ATT_SUB, 128), 1)
    n_sub = ATT_TK // ATT_SUB

    def accumulate(kj, masked):
        base = pl.multiple_of(kj * ATT_TK, ATT_TK)

        def first_query(c):
            return c * ATT_SUB if masked else 0

        def scores(c):
            k = k_ref[pl.ds(base + c * ATT_SUB, ATT_SUB), :]
            q0 = first_query(c)
            for m in range(2):
                in_map = (lane < DIFF_DH) if m == 0 else (lane >= DIFF_DH)
                km = jnp.where(in_map, k, jnp.zeros_like(k))
                s_ref[m, c, :, q0:] = jnp.dot(km, qt[:, q0:],
                                              preferred_element_type=F32)

        def softmax_pv(c):
            vt = vt_ref[:, pl.ds(base + c * ATT_SUB, ATT_SUB)]
            q0 = first_query(c)
            for m in range(2):
                s = s_ref[m, c, :, q0:]
                if masked:
                    key = lax.broadcasted_iota(jnp.int32, s.shape, 0)
                    qry = lax.broadcasted_iota(jnp.int32, s.shape, 1)
                    s = jnp.where(key <= qry, s, -jnp.inf)
                m_old = m_ref[m, :, q0:]
                m_new = jnp.maximum(m_old, jnp.max(s, axis=0, keepdims=True))
                p = jnp.exp2((s - m_new).astype(BF16))
                alpha = jnp.exp2(m_old - m_new)
                acc_ref[m, :, q0:] = (alpha * acc_ref[m, :, q0:]
                                      + jnp.dot(vt, p, preferred_element_type=F32))
                m_ref[m, :, q0:] = m_new

        scores(0)
        for c in range(n_sub):
            if c + 1 < n_sub:
                scores(c + 1)
            softmax_pv(c)

    def full_block(kj, carry):
        accumulate(kj, False)
        return carry

    lax.fori_loop(0, qi, full_block, 0)
    accumulate(qi, True)

    lv = lam_ref[...]
    lam = (jnp.exp(jnp.sum(lv[0:1] * lv[1:2], axis=1, keepdims=True))
           - jnp.exp(jnp.sum(lv[2:3] * lv[3:4], axis=1, keepdims=True)) + lambda_init)
    a0 = acc_ref[0]
    a1 = acc_ref[1]
    o = (a0[:DIFF_DV] / a0[DIFF_DV:DIFF_DV + 1]
         - lam * (a1[:DIFF_DV] / a1[DIFF_DV:DIFF_DV + 1]))
    y = o * lax.rsqrt(jnp.mean(o * o, axis=0, keepdims=True) + RMS_EPS)
    y = y * (gn_ref[0] * (1.0 - lambda_init))
    o_ref[...] = y.T.astype(o_ref.dtype)


def diff_flash(qt, k, vt, lam_vecs, g_norm, *, batch, seq, lambda_init):
    n = k.shape[0]
    nq = seq // ATT_TQ
    lam_tab = jnp.zeros((8, 128), F32).at[0:4, 0:DIFF_DH].set(lam_vecs)
    return pl.pallas_call(
        functools.partial(_flash_kernel, lambda_init=lambda_init),
        grid=(batch, DIFF_HEADS, nq),
        in_specs=[
            pl.BlockSpec((8, 128), lambda b, h, i: (0, 0)),
            pl.BlockSpec((128, ATT_TQ), lambda b, h, i: (h, b * nq + i)),
            pl.BlockSpec((seq, 128), lambda b, h, i: (b, h)),
            pl.BlockSpec((ATT_VROWS, seq), lambda b, h, i: (h, b)),
            pl.BlockSpec((1, DIFF_DV, 1), lambda b, h, i: (h, 0, 0)),
        ],
        out_specs=pl.BlockSpec((ATT_TQ, 128), lambda b, h, i: (b * nq + i, h)),
        out_shape=jax.ShapeDtypeStruct((n, DIFF_HEADS * DIFF_DV), BF16),
        scratch_shapes=[pltpu.VMEM((2, 1, ATT_TQ), F32),
                        pltpu.VMEM((2, ATT_VROWS, ATT_TQ), F32),
                        pltpu.VMEM((2, ATT_TK // ATT_SUB, ATT_SUB, ATT_TQ), F32)],
        compiler_params=_params("parallel", "parallel", "arbitrary"),
        name="diff_flash",
    )(lam_tab, qt, k, vt, g_norm.reshape(DIFF_HEADS, DIFF_DV, 1))


def _compare_exchange(x, i, j):
    hi, lo = jnp.maximum(x[i], x[j]), jnp.minimum(x[i], x[j])
    x[i], x[j] = hi, lo


def _bitonic_merge_desc(x):
    n = len(x)
    j = n // 2
    while j >= 1:
        for i in range(n):
            if i & j == 0:
                _compare_exchange(x, i, i | j)
        j //= 2


def _bitonic_sort_desc(x):
    n = len(x)
    k = 2
    while k <= n:
        j = k // 2
        while j >= 1:
            for i in range(n):
                l = i ^ j
                if l > i:
                    if i & k == 0:
                        _compare_exchange(x, i, l)
                    else:
                        _compare_exchange(x, l, i)
            j //= 2
        k *= 2


def _top16_per_column(slabs):
    x = list(slabs)
    n = len(x)
    _bitonic_sort_desc(x)
    for shift in (4, 2, 1):
        partner = [pltpu.roll(v, shift, 0) for v in x]
        x = [jnp.maximum(x[i], partner[n - 1 - i]) for i in range(n)]
        _bitonic_merge_desc(x)
    return [v[0:1, :] for v in x]


def _peer_select_kernel(q_ref, keys_ref, e1_ref, d1_ref, e2_ref, c2_ref):
    k_top = PEER_TOPK
    q = q_ref[...].astype(BF16)
    keys = keys_ref[0].astype(BF16)
    s1 = lax.dot_general(keys[0], q[:, :128], NT_DIMS, preferred_element_type=F32)
    s2 = lax.dot_general(keys[1], q[:, 128:], NT_DIMS, preferred_element_type=F32)
    a = _top16_per_column([s1[8 * g:8 * g + 8, :] for g in range(PEER_NKEYS // 8)])
    b = _top16_per_column([s2[8 * g:8 * g + 8, :] for g in range(PEER_NKEYS // 8)])
    pairs = [(i, j) for i in range(k_top) for j in range(k_top) if (i + 1) * (j + 1) <= k_top]
    cands = [a[i] + b[j] for i, j in pairs]
    neg = jnp.full_like(a[0], -jnp.inf)
    cands += [neg] * (4 * k_top - len(cands))
    slabs = [jnp.concatenate(cands[4 * g:4 * g + 4] + [neg] * 4, axis=0) for g in range(k_top)]
    best = _top16_per_column(slabs)
    top = best[0]
    z = jnp.exp(best[0] - top)
    for r in range(1, k_top):
        z = z + jnp.exp(best[r] - top)
    inv_z = 1.0 / z
    kth = best[k_top - 1]
    d1 = jnp.full_like(s1, float(k_top + 1))
    for i in range(k_top):
        n_i = sum(jnp.where(a[i] + b[j] >= kth, 1.0, 0.0) for (ii, j) in pairs if ii == i)
        d1 = jnp.where(s1 == a[i], float(k_top + 1) - n_i, d1)
    c2 = jnp.zeros_like(s2)
    for r in range(k_top):
        c2 = c2 + jnp.where(s2 >= b[r], 1.0, 0.0)
    e1_ref[0] = jnp.where(s1 >= a[k_top - 1], jnp.exp(s1 - a[0]), 0.0)
    d1_ref[0] = d1
    e2_ref[0] = jnp.where(c2 > 0.0, jnp.exp(s2 - b[0]) * inv_z, 0.0).astype(BF16)
    c2_ref[0] = c2.astype(BF16)


def peer_select(qry, sub_keys, *, tt):
    n = qry.shape[0]
    hh = PEER_HEADS
    spec = pl.BlockSpec((1, PEER_NKEYS, tt), lambda i, h: (h, 0, i))
    shape = lambda dt: jax.ShapeDtypeStruct((hh, PEER_NKEYS, n), dt)
    return pl.pallas_call(
        _peer_select_kernel,
        grid=(n // tt, hh),
        in_specs=[pl.BlockSpec((tt, 256), lambda i, h: (i, h)),
                  pl.BlockSpec((1, 2, PEER_NKEYS, 128), lambda i, h: (h, 0, 0, 0))],
        out_specs=[spec, spec, spec, spec],
        out_shape=[shape(F32), shape(F32), shape(BF16), shape(BF16)],
        compiler_params=_params("parallel", "arbitrary"),
        name="peer_select",
    )(qry, sub_keys)


PEER_TT = 512
PEER_ET = 1024
PEER_MIX = 256
BF16_ROWS = 16


def _peer_dense_kernel(xt_ref, u_ref, vt_ref, e1_ref, d1_ref, e2_ref, c2_ref, o_ref,
                       acts_ref, g_ref):
    @pl.when(pl.program_id(1) == 0)
    def _():
        o_ref[...] = jnp.zeros_like(o_ref)

    xt = xt_ref[...]
    n_rows = PEER_ET // PEER_NKEYS

    def activations(r):
        sl = slice(r * PEER_NKEYS, (r + 1) * PEER_NKEYS)
        acts_ref[sl, :] = jnp.dot(u_ref[sl, :], xt, preferred_element_type=F32)

    def gates(r):
        tiles = (PEER_NKEYS // BF16_ROWS, BF16_ROWS, PEER_TT)
        w = None
        for h in range(PEER_HEADS):
            e1 = jnp.broadcast_to(e1_ref[h, r:r + 1, :], tiles[1:]).astype(BF16)
            d1 = jnp.broadcast_to(d1_ref[h, r:r + 1, :], tiles[1:]).astype(BF16)
            e2 = e2_ref[h].reshape(tiles)
            c2 = c2_ref[h].reshape(tiles)
            p = e1[None] * e2
            c = jnp.where(c2 >= jnp.broadcast_to(d1[None], tiles), p, jnp.zeros_like(p))
            w = c if w is None else w + c
        return w.reshape(PEER_NKEYS, PEER_TT)

    def times_gelu(r, w):
        sl = slice(r * PEER_NKEYS, (r + 1) * PEER_NKEYS)
        a = acts_ref[sl, :]
        gelu = 0.5 * a * (1.0 + lax.erf(a * (2.0 ** -0.5)))
        g_ref[sl, :] = w * gelu.astype(BF16)

    def mix(q):
        sl = slice(q * PEER_MIX, (q + 1) * PEER_MIX)
        o_ref[...] += jnp.dot(vt_ref[:, sl], g_ref[sl, :], preferred_element_type=F32)

    per_mix = PEER_MIX // PEER_NKEYS
    activations(0)
    w = gates(0)
    activations(1)
    for r in range(n_rows):
        if r + 2 < n_rows:
            activations(r + 2)
        times_gelu(r, w)
        if r + 1 < n_rows:
            w = gates(r + 1)
        if (r + 1) % per_mix == 0:
            mix(r // per_mix)


def peer_dense(xn_t, u, v_t, e1, d1, e2, c2):
    d, n = xn_t.shape
    ne = u.shape[0]
    rows = PEER_ET // PEER_NKEYS
    per_row = pl.BlockSpec((PEER_HEADS, rows, PEER_TT), lambda i, j: (0, j, i))
    per_col = pl.BlockSpec((PEER_HEADS, PEER_NKEYS, PEER_TT), lambda i, j: (0, 0, i))
    return pl.pallas_call(
        _peer_dense_kernel,
        grid=(n // PEER_TT, ne // PEER_ET),
        in_specs=[pl.BlockSpec((d, PEER_TT), lambda i, j: (0, i)),
                  pl.BlockSpec((PEER_ET, d), lambda i, j: (j, 0)),
                  pl.BlockSpec((d, PEER_ET), lambda i, j: (0, j)),
                  per_row, per_row, per_col, per_col],
        out_specs=pl.BlockSpec((d, PEER_TT), lambda i, j: (0, i)),
        out_shape=jax.ShapeDtypeStruct((d, n), F32),
        scratch_shapes=[pltpu.VMEM((PEER_ET, PEER_TT), F32),
                        pltpu.VMEM((PEER_ET, PEER_TT), BF16)],
        compiler_params=pltpu.CompilerParams(
            dimension_semantics=("parallel", "arbitrary"),
            vmem_limit_bytes=V7X_VMEM_LIMIT_BYTES,
            allow_input_fusion=[True, True, True, False, False, False, False]),
        name="peer_dense",
    )(xn_t, u, v_t, e1, d1, e2, c2)


def _add_t_kernel(h_ref, ft_ref, g_ref, o_ref, *, final_norm):
    y = h_ref[...] + ft_ref[...].T
    if final_norm:
        y = y * lax.rsqrt(jnp.mean(y * y, axis=-1, keepdims=True) + RMS_EPS) * g_ref[...]
    o_ref[...] = y


def add_transposed(h, ff_t, g, *, tm, final_norm):
    n, d = h.shape
    return pl.pallas_call(
        functools.partial(_add_t_kernel, final_norm=final_norm),
        grid=(n // tm,),
        in_specs=[pl.BlockSpec((tm, d), lambda i: (i, 0)),
                  pl.BlockSpec((d, tm), lambda i: (0, i)),
                  pl.BlockSpec((1, d), lambda i: (0, 0))],
        out_specs=pl.BlockSpec((tm, d), lambda i: (i, 0)),
        out_shape=jax.ShapeDtypeStruct((n, d), F32),
        compiler_params=_params("parallel"),
        name="add_t_norm" if final_norm else "add_t",
    )(h, ff_t, g.reshape(1, d))


def _peer_layer(h, g_ffn, w_query, sub_keys, expert_u, expert_v, g_after, final_norm):
    qry, xn = rms_matmul(h, g_ffn, w_query.astype(BF16), tm=1024, tn=1024, emit_xn=True)
    e1, d1, e2, c2 = peer_select(qry, sub_keys, tt=512)
    ff_t = peer_dense(xn.T, expert_u.astype(BF16), expert_v.astype(BF16).T, e1, d1, e2, c2)
    return add_transposed(h, ff_t, g_after, tm=512, final_norm=final_norm)


def _gla_layer(h, g_mix, w_in, w_gate_up, b_gate, g_norm, w_out, *, batch, seq):
    pad = GLA_IN_PAD - w_in.shape[1]
    w_in_pad = jnp.pad(w_in, ((0, 0), (0, pad))).astype(BF16)
    w_gate_pad = jnp.pad(w_gate_up, ((0, 128 - GLA_RANK), (0, 0)))
    proj = rms_matmul(h, g_mix, w_in_pad, tm=1024, tn=640)
    o = gla_core(proj, w_gate_pad, b_gate, g_norm, batch=batch, seq=seq)
    return matmul_res(o, w_out.astype(BF16), h, tm=512, tn=1024)


def _diff_layer(h, positions, g_mix, w_in, lam_vecs, g_norm, w_out, lambda_init, *, batch, seq):
    proj = rms_matmul(h, g_mix, w_in.astype(BF16), tm=1024, tn=1024)
    qt, k, vt = rope_prep(proj, positions.reshape(-1), tm=512)
    o = diff_flash(qt, k, vt, lam_vecs, g_norm, batch=batch, seq=seq, lambda_init=lambda_init)
    return matmul_res(o, w_out.astype(BF16), h, tm=512, tn=1024)


def kernel(x, positions, norm_mix, norm_ffn, norm_final, gla_w_in, gla_w_gate_up, gla_b_gate, gla_norm, gla_w_out, diff_w_in, diff_lambda_q1, diff_lambda_k1, diff_lambda_q2, diff_lambda_k2, diff_norm, diff_w_out, peer_w_query, peer_sub_keys, peer_u, peer_v):
    batch, seq, d = x.shape
    depth = norm_mix.shape[0]
    h = x.reshape(batch * seq, d)
    for i in range(depth):
        j = i // 2
        if i % 2 == 0:
            h = _gla_layer(h, norm_mix[i], gla_w_in[j], gla_w_gate_up[j], gla_b_gate[j],
                           gla_norm[j], gla_w_out[j], batch=batch, seq=seq)
        else:
            lambda_init = 0.8 - 0.6 * math.exp(-0.3 * i)
            lam_vecs = jnp.stack([diff_lambda_q1[j], diff_lambda_k1[j],
                                  diff_lambda_q2[j], diff_lambda_k2[j]])
            h = _diff_layer(h, positions, norm_mix[i], diff_w_in[j], lam_vecs, diff_norm[j],
                            diff_w_out[j], lambda_init, batch=batch, seq=seq)
        last = i == depth - 1
        h = _peer_layer(h, norm_ffn[i], peer_w_query[i], peer_sub_keys[i], peer_u[i], peer_v[i],
                        norm_final, final_norm=last)
    return h.reshape(batch, seq, d)
```
